```python
import math
import jax
import jax.numpy as jnp
from jax import lax
import numpy as np

D_MODEL = 1024
BATCH = 16
SEQ = 256
DEPTH = 4
DEC_BATCH = 8
DEC_SEQ = 4096
PAST_LEN = 256

GRID_W = 64
N_MIXERS = 3
EPS = 1e-6
CONV_W = 4

SSD_INNER = 2 * D_MODEL
SSD_HEAD_DIM = 64
SSD_HEADS = SSD_INNER // SSD_HEAD_DIM
SSD_GROUPS = 8
SSD_STATE = 128
SSD_CHUNK = 128
SSD_CONV_CH = SSD_INNER + 2 * SSD_GROUPS * SSD_STATE
SSD_IN = SSD_INNER + SSD_CONV_CH + 2 * SSD_HEADS

GLA_HEADS = 4
GLA_KEY = D_MODEL // 2
GLA_VAL = D_MODEL
GLA_DK = GLA_KEY // GLA_HEADS
GLA_DV = GLA_VAL // GLA_HEADS
GLA_RANK = 16
GLA_TAU = 16.0
GLA_CHUNK = 64
GLA_IN = 2 * GLA_KEY + 2 * GLA_VAL + 2 * GLA_RANK

LRU_WIDTH = D_MODEL
LRU_BLOCKS = 8
LRU_BLOCK = LRU_WIDTH // LRU_BLOCKS
LRU_C = 8.0

D_FF = 2816
N_EXPERTS = 8
TOP_K = 2

N_SSD_LAYERS = (DEPTH + 2) // 3
N_GLA_LAYERS = (DEPTH + 1) // 3
N_LRU_LAYERS = DEPTH // 3
N_DENSE_LAYERS = (DEPTH + 1) // 2
N_MOE_LAYERS = DEPTH // 2

kernel_name = "hybrid_ssd_gla_rglru_flow_step"


def rms_norm(x, g):
    xf = x.astype(jnp.float32)
    y = xf * lax.rsqrt(jnp.mean(xf * xf, axis=-1, keepdims=True) + EPS)
    return (y * g.astype(jnp.float32)).astype(x.dtype)


def dwconv_centred(x, w, b, n_seg):
    bsz, L, ch = x.shape
    seg = L // n_seg
    xs = x.reshape(bsz * n_seg, seg, ch)
    lo = (CONV_W - 1) // 2
    xp = jnp.pad(xs, ((0, 0), (lo, CONV_W - 1 - lo), (0, 0)))
    y = b
    for k in range(CONV_W):
        y = y + xp[:, k:k + seg] * w[k]
    return y.reshape(bsz, L, ch)


def ssd_scan(x, log_a, b, c, h0):
    bsz, L, H, P = x.shape
    G, N = b.shape[2], b.shape[3]
    R = H // G
    Q = SSD_CHUNK
    nc = L // Q
    xc = x.reshape(bsz, nc, Q, G, R, P)
    ac = log_a.reshape(bsz, nc, Q, G, R)
    bc = b.reshape(bsz, nc, Q, G, N)
    cc = c.reshape(bsz, nc, Q, G, N)
    acum = jnp.cumsum(ac, axis=2)
    seg = acum[:, :, :, None] - acum[:, :, None, :]
    causal = jnp.tril(jnp.ones((Q, Q), dtype=bool))[:, :, None, None]
    decay = jnp.exp(jnp.where(causal, seg, -jnp.inf))
    cb = jnp.einsum('bcqgn,bcsgn->bcqsg', cc, bc)
    y_diag = jnp.einsum('bcqsg,bcqsgr,bcsgrp->bcqgrp', cb, decay, xc)
    decay_to_end = jnp.exp(acum[:, :, -1:] - acum)
    states = jnp.einsum('bcsgn,bcsgr,bcsgrp->bcgrpn', bc, decay_to_end, xc)
    chunk_decay = jnp.exp(acum[:, :, -1])

    def step(h, inp):
        st, dec = inp
        return h * dec[..., None, None] + st, h

    h_fin, h_prev = lax.scan(step, h0.reshape(bsz, G, R, P, N),
                             (jnp.moveaxis(states, 1, 0), jnp.moveaxis(chunk_decay, 1, 0)))
    h_prev = jnp.moveaxis(h_prev, 0, 1)
    y_off = jnp.einsum('bcqgn,bcqgr,bcgrpn->bcqgrp', cc, jnp.exp(acum), h_prev)
    y = (y_diag + y_off).reshape(bsz, L, H, P)
    return y, h_fin.reshape(bsz, H, P, N)


def ssd_mixer(u, h0, n_seg, w_in, conv_w, conv_b, a_log, dt_bias, d_skip, norm_g, w_out):
    bsz, L, _ = u.shape
    f32 = jnp.float32
    proj = u @ w_in
    z, xbc, dt = jnp.split(proj, [SSD_INNER, SSD_INNER + SSD_CONV_CH], axis=-1)
    xbc = jax.nn.silu(dwconv_centred(xbc, conv_w, conv_b, n_seg))
    xs, bs, cs = jnp.split(xbc, [SSD_INNER, SSD_INNER + SSD_GROUPS * SSD_STATE], axis=-1)
    xh = xs.reshape(bsz, L, SSD_HEADS, SSD_HEAD_DIM).astype(f32)
    bg = bs.reshape(bsz, L, SSD_GROUPS, SSD_STATE).astype(f32)
    cg = cs.reshape(bsz, L, SSD_GROUPS, SSD_STATE).astype(f32)
    dt = jax.nn.softplus(dt.reshape(bsz, L, 2, SSD_HEADS).astype(f32) + dt_bias.astype(f32))
    log_a = -dt * jnp.exp(a_log.astype(f32))
    xdt = xh[:, :, None] * dt[..., None]
    h0 = h0.astype(f32)
    y_f, h_f = ssd_scan(xdt[:, :, 0], log_a[:, :, 0], bg, cg, h0[:, 0])
    y_b, h_b = ssd_scan(jnp.flip(xdt[:, :, 1], 1), jnp.flip(log_a[:, :, 1], 1),
                        jnp.flip(bg, 1), jnp.flip(cg, 1), h0[:, 1])
    y = y_f + jnp.flip(y_b, 1) + d_skip.astype(f32)[:, None] * xh
    y = y.reshape(bsz, L, SSD_INNER).astype(u.dtype)
    y = rms_norm(y * jax.nn.silu(z), norm_g)
    return y @ w_out, jnp.stack([h_f, h_b], axis=1)


def gla_scan(q, k, v, log_alpha, s0):
    bsz, L, H, K = q.shape
    V = v.shape[-1]
    Q = GLA_CHUNK
    nc = L // Q
    qc = q.reshape(bsz, nc, Q, H, K)
    kc = k.reshape(bsz, nc, Q, H, K)
    gc = log_alpha.reshape(bsz, nc, Q, H, K)
    vc = v.reshape(bsz, nc, Q, H, V)
    gcum = jnp.cumsum(gc, axis=2)
    q_i = qc * jnp.exp(gcum)
    k_i = kc * jnp.exp(-gcum)
    causal = jnp.tril(jnp.ones((Q, Q), dtype=bool))
    att = jnp.where(causal, jnp.einsum('bcqhk,bcshk->bchqs', q_i, k_i), 0.0)
    o_intra = jnp.einsum('bchqs,bcshv->bcqhv', att, vc)
    g_last = gcum[:, :, -1]
    k_dec = kc * jnp.exp(g_last[:, :, None] - gcum)
    s_chunk = jnp.einsum('bcshk,bcshv->bchkv', k_dec, vc)

    def step(s, inp):
        sc, dec = inp
        return s * dec[..., None] + sc, s

    s_fin, s_prev = lax.scan(step, s0,
                             (jnp.moveaxis(s_chunk, 1, 0), jnp.moveaxis(jnp.exp(g_last), 1, 0)))
    s_prev = jnp.moveaxis(s_prev, 0, 1)
    o_inter = jnp.einsum('bcqhk,bchkv->bcqhv', q_i, s_prev)
    return (o_intra + o_inter).reshape(bsz, L, H, V), s_fin


def gla_mixer(u, s0, w_in, w_gate_up, b_gate, norm_g, w_out):
    bsz, L, _ = u.shape
    f32 = jnp.float32
    proj = u @ w_in
    q, k, v, r, gdown = jnp.split(
        proj, [GLA_KEY, 2 * GLA_KEY, 2 * GLA_KEY + GLA_VAL, 2 * GLA_KEY + 2 * GLA_VAL], axis=-1)
    q = q.reshape(bsz, L, GLA_HEADS, GLA_DK).astype(f32) * (GLA_DK ** -0.5)
    k = k.reshape(bsz, L, GLA_HEADS, GLA_DK).astype(f32)
    v = v.reshape(bsz, L, GLA_HEADS, GLA_DV).astype(f32)
    gdown = gdown.reshape(bsz, L, 2, GLA_RANK)
    gate_logit = jnp.einsum('blir,irk->blik', gdown, w_gate_up) + b_gate
    la = (jax.nn.log_sigmoid(gate_logit.astype(f32)) / GLA_TAU).reshape(bsz, L, 2, GLA_HEADS, GLA_DK)
    s0 = s0.astype(f32)
    o_f, s_f = gla_scan(q, k, v, la[:, :, 0], s0[:, 0])
    o_b, s_b = gla_scan(jnp.flip(q, 1), jnp.flip(k, 1), jnp.flip(v, 1),
                        jnp.flip(la[:, :, 1], 1), s0[:, 1])
    o = (o_f + jnp.flip(o_b, 1)).astype(u.dtype)
    o = rms_norm(o, norm_g).reshape(bsz, L, GLA_VAL) * jax.nn.silu(r)
    return o @ w_out, jnp.stack([s_f, s_b], axis=1)


def linear_scan(a, b, h0, reverse):
    idx = -1 if reverse else 0
    b = b.at[:, idx].add(a[:, idx] * h0)

    def comb(left, right):
        a_l, b_l = left
        a_r, b_r = right
        return a_l * a_r, a_r * b_l + b_r

    _, h = lax.associative_scan(comb, (a, b), reverse=reverse, axis=1)
    return h


def lru_mixer(u, h0, n_seg, w_in, conv_w, conv_b, w_a, b_a, w_x, b_x, lam, w_out):
    bsz, L, _ = u.shape
    f32 = jnp.float32
    proj = u @ w_in
    gate, xb = jnp.split(proj, [LRU_WIDTH], axis=-1)
    xb = dwconv_centred(xb, conv_w, conv_b, n_seg).astype(f32)
    xblk = xb.reshape(bsz, L, LRU_BLOCKS, LRU_BLOCK)
    r = jax.nn.sigmoid(jnp.einsum('blnc,dncm->bldnm', xblk, w_a.astype(f32)).reshape(bsz, L, 2, LRU_WIDTH)
                       + b_a.astype(f32))
    i = jax.nn.sigmoid(jnp.einsum('blnc,dncm->bldnm', xblk, w_x.astype(f32)).reshape(bsz, L, 2, LRU_WIDTH)
                       + b_x.astype(f32))
    log_a = -LRU_C * r * jax.nn.softplus(-lam.astype(f32))
    a = jnp.exp(log_a)
    bterm = jnp.sqrt(-jnp.expm1(2.0 * log_a)) * (i * xb[:, :, None])
    h0 = h0.astype(f32)
    hf = linear_scan(a[:, :, 0], bterm[:, :, 0], h0[:, 0], reverse=False)
    hb = linear_scan(a[:, :, 1], bterm[:, :, 1], h0[:, 1], reverse=True)
    y = (hf + hb).astype(u.dtype) * jax.nn.gelu(gate)
    return y @ w_out, jnp.stack([hf[:, -1], hb[:, 0]], axis=1)


def swiglu(x, w1, w3, w2):
    return (jax.nn.silu(x @ w1) * (x @ w3)) @ w2


def moe_swiglu(x, w_router, w1, w3, w2):
    logits = (x @ w_router).astype(jnp.float32)
    top_val, top_idx = lax.top_k(logits, TOP_K)
    gates = jax.nn.softmax(top_val, axis=-1)
    combine = jnp.sum(jax.nn.one_hot(top_idx, N_EXPERTS, dtype=jnp.float32) * gates[..., None], axis=-2)
    combine = combine.astype(x.dtype)
    out = jnp.zeros_like(x)
    for e in range(N_EXPERTS):
        out = out + combine[..., e:e + 1] * swiglu(x, w1[e], w3[e], w2[e])
    return out


def setup_inputs(seed: int = 0) -> dict:
    key = jax.random.key(seed)
    ks = iter(jax.random.split(key, 64))
    f32 = jnp.float32

    def nrm(shape, scale):
        return jax.random.normal(next(ks), shape, f32) * scale

    def unif(shape, lo, hi):
        return jax.random.uniform(next(ks), shape, f32, lo, hi)

    NS, NG, NL = N_SSD_LAYERS, N_GLA_LAYERS, N_LRU_LAYERS
    ND, NM = N_DENSE_LAYERS, N_MOE_LAYERS
    dt0 = jnp.exp(unif((NS, 2, SSD_HEADS), math.log(1e-3), math.log(1e-1)))
    a0 = unif((NL, 2, LRU_WIDTH), 0.9, 0.999)
    return {
        "x_prompt": nrm((BATCH, SEQ, D_MODEL), 1.0),
        "x_sample": nrm((DEC_BATCH, DEC_SEQ, D_MODEL), 1.0),
        "state_ssd": nrm((DEC_BATCH, NS, 2, SSD_HEADS, SSD_HEAD_DIM, SSD_STATE), 0.5),
        "state_gla": nrm((DEC_BATCH, NG, 2, GLA_HEADS, GLA_DK, GLA_DV), 0.5),
        "state_lru": nrm((DEC_BATCH, NL, 2, LRU_WIDTH), 0.5),
        "c": nrm((DEC_BATCH, D_MODEL), 1.0),
        "c_ctx": nrm((D_MODEL,), 1.0),
        "ada_w": nrm((DEPTH, D_MODEL, 6 * D_MODEL), 0.5 * D_MODEL ** -0.5),
        "ada_b": nrm((DEPTH, 6 * D_MODEL), 0.02),
        "norm_g": 1.0 + nrm((DEPTH, 2, D_MODEL), 0.02),
        "final_g": 1.0 + nrm((D_MODEL,), 0.02),
        "ssd_w_in": nrm((NS, D_MODEL, SSD_IN), D_MODEL ** -0.5),
        "ssd_conv_w": nrm((NS, CONV_W, SSD_CONV_CH), CONV_W ** -0.5),
        "ssd_conv_b": nrm((NS, SSD_CONV_CH), 0.02),
        "ssd_a_log": jnp.log(unif((NS, 2, SSD_HEADS), 1.0, 16.0)),
        "ssd_dt_bias": dt0 + jnp.log(-jnp.expm1(-dt0)),
        "ssd_d": 1.0 + nrm((NS, SSD_HEADS), 0.1),
        "ssd_norm_g": 1.0 + nrm((NS, SSD_INNER), 0.02),
        "ssd_w_out": nrm((NS, SSD_INNER, D_MODEL), SSD_INNER ** -0.5),
        "gla_w_in": nrm((NG, D_MODEL, GLA_IN), D_MODEL ** -0.5),
        "gla_w_gate_up": nrm((NG, 2, GLA_RANK, GLA_KEY), GLA_RANK ** -0.5),
        "gla_b_gate": nrm((NG, 2, GLA_KEY), 0.1),
        "gla_norm_g": 1.0 + nrm((NG, GLA_DV), 0.02),
        "gla_w_out": nrm((NG, GLA_VAL, D_MODEL), GLA_VAL ** -0.5),
        "lru_w_in": nrm((NL, D_MODEL, 2 * LRU_WIDTH), D_MODEL ** -0.5),
        "lru_conv_w": nrm((NL, CONV_W, LRU_WIDTH), CONV_W ** -0.5),
        "lru_conv_b": nrm((NL, LRU_WIDTH), 0.02),
        "lru_w_a": nrm((NL, 2, LRU_BLOCKS, LRU_BLOCK, LRU_BLOCK), LRU_BLOCK ** -0.5),
        "lru_b_a": nrm((NL, 2, LRU_WIDTH), 0.02),
        "lru_w_x": nrm((NL, 2, LRU_BLOCKS, LRU_BLOCK, LRU_BLOCK), LRU_BLOCK ** -0.5),
        "lru_b_x": nrm((NL, 2, LRU_WIDTH), 0.02),
        "lru_lambda": jnp.log(a0) - jnp.log1p(-a0),
        "lru_w_out": nrm((NL, LRU_WIDTH, D_MODEL), LRU_WIDTH ** -0.5),
        "ffn_w1": nrm((ND, D_MODEL, D_FF), D_MODEL ** -0.5),
        "ffn_w3": nrm((ND, D_MODEL, D_FF), D_MODEL ** -0.5),
        "ffn_w2": nrm((ND, D_FF, D_MODEL), D_FF ** -0.5),
        "moe_router": nrm((NM, D_MODEL, N_EXPERTS), D_MODEL ** -0.5),
        "moe_w1": nrm((NM, N_EXPERTS, D_MODEL, D_FF), D_MODEL ** -0.5),
        "moe_w3": nrm((NM, N_EXPERTS, D_MODEL, D_FF), D_MODEL ** -0.5),
        "moe_w2": nrm((NM, N_EXPERTS, D_FF, D_MODEL), D_FF ** -0.5),
    }


def reference(x_prompt, x_sample, state_ssd, state_gla, state_lru, c, c_ctx,
              ada_w, ada_b, norm_g, final_g,
              ssd_w_in, ssd_conv_w, ssd_conv_b, ssd_a_log, ssd_dt_bias, ssd_d, ssd_norm_g, ssd_w_out,
              gla_w_in, gla_w_gate_up, gla_b_gate, gla_norm_g, gla_w_out,
              lru_w_in, lru_conv_w, lru_conv_b, lru_w_a, lru_b_a, lru_w_x, lru_b_x, lru_lambda, lru_w_out,
              ffn_w1, ffn_w3, ffn_w2,
              moe_router, moe_w1, moe_w3, moe_w2):
    rows = x_sample.shape[1] // GRID_W
    bsz_ctx = x_prompt.shape[0]
    f32 = jnp.float32

    def token_mixer(i, u, h0, n_seg):
        j = i // N_MIXERS
        kind = i % N_MIXERS
        if kind == 0:
            return ssd_mixer(u, h0, n_seg, ssd_w_in[j], ssd_conv_w[j], ssd_conv_b[j], ssd_a_log[j],
                             ssd_dt_bias[j], ssd_d[j], ssd_norm_g[j], ssd_w_out[j])
        if kind == 1:
            return gla_mixer(u, h0, gla_w_in[j], gla_w_gate_up[j], gla_b_gate[j],
                             gla_norm_g[j], gla_w_out[j])
        return lru_mixer(u, h0, n_seg, lru_w_in[j], lru_conv_w[j], lru_conv_b[j], lru_w_a[j],
                         lru_b_a[j], lru_w_x[j], lru_b_x[j], lru_lambda[j], lru_w_out[j])

    def channel_mixer(i, u):
        j = i // 2
        if i % 2 == 0:
            return swiglu(u, ffn_w1[j], ffn_w3[j], ffn_w2[j])
        return moe_swiglu(u, moe_router[j], moe_w1[j], moe_w3[j], moe_w2[j])

    def layer(i, x, mod, h0, n_seg):
        sh1, sc1, gt1, sh2, sc2, gt2 = jnp.split(mod, 6, axis=-1)
        u = rms_norm(x, norm_g[i, 0]) * (1.0 + sc1) + sh1
        y, h_fin = token_mixer(i, u, h0, n_seg)
        x = x + gt1 * y
        u = rms_norm(x, norm_g[i, 1]) * (1.0 + sc2) + sh2
        x = x + gt2 * channel_mixer(i, u)
        return x, h_fin

    zero_states = (
        jnp.zeros((bsz_ctx, 2, SSD_HEADS, SSD_HEAD_DIM, SSD_STATE), f32),
        jnp.zeros((bsz_ctx, 2, GLA_HEADS, GLA_DK, GLA_DV), f32),
        jnp.zeros((bsz_ctx, 2, LRU_WIDTH), f32),
    )
    caches = (state_ssd, state_gla, state_lru)
    ctx_states = ([], [], [])
    xp, xs = x_prompt, x_sample
    for i in range(DEPTH):
        kind, j = i % N_MIXERS, i // N_MIXERS
        mod_ctx = (jax.nn.silu(c_ctx) @ ada_w[i] + ada_b[i])[None, None]
        mod_lat = (jax.nn.silu(c) @ ada_w[i] + ada_b[i])[:, None]
        xp, h_ctx = layer(i, xp, mod_ctx, zero_states[kind], 1)
        ctx_states[kind].append(h_ctx)
        xs, _ = layer(i, xs, mod_lat, caches[kind][:, j], rows)

    y_prompt = rms_norm(xp, final_g)
    y_sample = rms_norm(xs, final_g)
    new_state_ssd = jnp.stack(ctx_states[0], axis=1)
    new_state_gla = jnp.stack(ctx_states[1], axis=1)
    new_state_lru = jnp.stack(ctx_states[2], axis=1)
    return (y_prompt, y_sample, new_state_ssd, new_state_gla, new_state_lru)
```

```python
import collections
import functools

import jax
import jax.numpy as jnp
import numpy as np
from jax import lax
from jax.experimental import pallas as pl
from jax.experimental.pallas import tpu as pltpu

F32 = jnp.float32
BF16 = jnp.bfloat16
I32 = jnp.int32

D_MODEL = 1024
DEPTH = 4
GRID_W = 64
N_MIXERS = 3
EPS = 1e-6
CONV_W = 4

SSD_INNER = 2 * D_MODEL
SSD_HEAD_DIM = 64
SSD_HEADS = SSD_INNER // SSD_HEAD_DIM
SSD_GROUPS = 8
SSD_GROUP_HEADS = SSD_HEADS // SSD_GROUPS
SSD_GROUP_W = SSD_GROUP_HEADS * SSD_HEAD_DIM
SSD_STATE = 128
SSD_CHUNK = 128
SSD_BC = SSD_GROUPS * SSD_STATE
SSD_CONV_CH = SSD_INNER + 2 * SSD_BC

GLA_HEADS = 4
GLA_KEY = D_MODEL // 2
GLA_VAL = D_MODEL
GLA_DK = GLA_KEY // GLA_HEADS
GLA_DV = GLA_VAL // GLA_HEADS
GLA_RANK = 16
GLA_TAU = 16.0
GLA_CHUNK = 64

LRU_WIDTH = D_MODEL
LRU_BLOCKS = 8
LRU_BLOCK = LRU_WIDTH // LRU_BLOCKS
LRU_C = 8.0

D_FF = 2816
N_EXPERTS = 8

LANE = 128
SUBLANE = 8
TM = 256
SCAN_BLK = 256
VMEM_LIMIT = 56 * 1024 * 1024

Layout = collections.namedtuple("Layout", "n_ctx len_ctx n_lat len_lat")


def _cparams(n_axes=1):
    return pltpu.CompilerParams(
        dimension_semantics=("arbitrary",) * n_axes, vmem_limit_bytes=VMEM_LIMIT)


def _silu(x):
    return x * jax.nn.sigmoid(x)


def _softplus(x):
    return jnp.maximum(x, 0.0) + jnp.log(1.0 + jnp.exp(-jnp.abs(x)))


def _rms(x, g):
    return x * lax.rsqrt(jnp.mean(x * x, axis=-1, keepdims=True) + EPS) * g


def _dot(a, b):
    return jnp.dot(a, b, preferred_element_type=F32)


def _dot_nt(a, b):
    return lax.dot_general(a, b, (((1,), (1,)), ((), ())), preferred_element_type=F32)


def _dot_tn(a, b):
    return lax.dot_general(a, b, (((0,), (0,)), ((), ())), preferred_element_type=F32)


def _split3(x):
    hi = x.astype(BF16)
    r1 = x - hi.astype(F32)
    mid = r1.astype(BF16)
    lo = (r1 - mid.astype(F32)).astype(BF16)
    return hi, mid, lo


def _tri_cumsum(tri, x):
    hi, mid, lo = _split3(x)
    return _dot(tri, hi) + _dot(tri, mid) + _dot(tri, lo)


class _Tiles:
    def __init__(self, lay, tm):
        assert lay.len_ctx % tm == 0 and lay.len_lat % tm == 0
        self.tm = tm
        self.nct = lay.n_ctx * lay.len_ctx // tm
        self.tpl = lay.len_lat // tm
        self.n = self.nct + lay.n_lat * self.tpl
        self.T = self.n * tm

    def mod_row(self, i):
        return jnp.where(i < self.nct, 0, 1 + (i - self.nct) // self.tpl)


def _full(shape):
    nd = len(shape)
    return pl.BlockSpec(shape, lambda *_: (0,) * nd)


def _mod_spec(tl):
    return pl.BlockSpec((None, 6, D_MODEL), lambda i, *_: (tl.mod_row(i), 0, 0))


def _row_spec(tl, width):
    return pl.BlockSpec((tl.tm, width), lambda i, *_: (i, 0))


def _mods_kernel(c_ref, w_ref, b_ref, o_ref):
    c = c_ref[...]
    a = _silu(c)
    o_ref[...] = jnp.dot(a, w_ref[...], preferred_element_type=F32,
                         precision=lax.Precision.HIGHEST) + b_ref[...]


def _mods(c_all, ada_w, ada_b):
    R = c_all.shape[0]
    nb = 6
    out = pl.pallas_call(
        _mods_kernel,
        grid=(DEPTH, nb),
        in_specs=[
            pl.BlockSpec((R, D_MODEL), lambda l, j: (0, 0)),
            pl.BlockSpec((None, D_MODEL, D_MODEL), lambda l, j: (l, 0, j)),
            pl.BlockSpec((None, 1, D_MODEL), lambda l, j: (l, 0, j)),
        ],
        out_specs=pl.BlockSpec((None, R, D_MODEL), lambda l, j: (l, 0, j)),
        out_shape=jax.ShapeDtypeStruct((DEPTH, R, 6 * D_MODEL), F32),
        compiler_params=_cparams(2),
        name="mods",
    )(c_all, ada_w, ada_b.reshape(DEPTH, 1, 6 * D_MODEL))
    return out.reshape(DEPTH, R, 6, D_MODEL)


def _conv_masks(tm, seg):
    r = lax.broadcasted_iota(I32, (tm, 1), 0) & (seg - 1)
    return r >= 1, r < seg - 1, r < seg - 2


def _conv4(p, cw, cb, masks):
    tm = p.shape[0]
    m_m1, m_p1, m_p2 = masks
    y = cb + cw[1:2] * p
    y = y + cw[0:1] * jnp.where(m_m1, pltpu.roll(p, 1, 0), 0.0)
    y = y + cw[2:3] * jnp.where(m_p1, pltpu.roll(p, tm - 1, 0), 0.0)
    y = y + cw[3:4] * jnp.where(m_p2, pltpu.roll(p, tm - 2, 0), 0.0)
    return y


def _ssd_in_kernel(x_ref, mod_ref, g_ref, w_ref, wdt_ref, cw_ref, cb_ref, dtb_ref,
                   z_ref, xs_ref, b_ref, c_ref, dt_ref, *, nct, seg_ctx, seg_lat):
    i = pl.program_id(0)
    tm = x_ref.shape[0]
    mod = mod_ref[...]
    u = (_rms(x_ref[...], g_ref[...]) * (1.0 + mod[1:2]) + mod[0:1]).astype(BF16)
    z_ref[...] = _dot(u, w_ref[:, :SSD_INNER]).astype(z_ref.dtype)
    masks = _conv_masks(tm, jnp.where(i < nct, seg_ctx, seg_lat))
    ch = 512
    for c0 in range(0, SSD_CONV_CH, ch):
        p = _dot(u, w_ref[:, SSD_INNER + c0:SSD_INNER + c0 + ch])
        y = _silu(_conv4(p, cw_ref[:, c0:c0 + ch], cb_ref[:, c0:c0 + ch], masks))
        if c0 < SSD_INNER:
            dst, base, w = xs_ref, c0, SSD_GROUP_W
        elif c0 < SSD_INNER + SSD_BC:
            dst, base, w = b_ref, c0 - SSD_INNER, SSD_STATE
        else:
            dst, base, w = c_ref, c0 - SSD_INNER - SSD_BC, SSD_STATE
        for k in range(ch // w):
            dst[base // w + k] = y[:, k * w:(k + 1) * w].astype(dst.dtype)
    dt_ref[...] = _softplus(_dot(u, wdt_ref[...]) + dtb_ref[...])


def _ssd_in(tl, lay, x, mod, g, w_main, w_dt, conv_w, conv_b, dt_bias):
    T = tl.T
    kern = functools.partial(_ssd_in_kernel, nct=tl.nct, seg_ctx=lay.len_ctx, seg_lat=GRID_W)
    gspec = lambda w: pl.BlockSpec((SSD_GROUPS, tl.tm, w), lambda i: (0, i, 0))
    return pl.pallas_call(
        kern,
        grid=(tl.n,),
        in_specs=[
            _row_spec(tl, D_MODEL), _mod_spec(tl), _full((1, D_MODEL)),
            _full(w_main.shape), _full(w_dt.shape), _full(conv_w.shape),
            _full(conv_b.shape), _full(dt_bias.shape),
        ],
        out_specs=[
            _row_spec(tl, SSD_INNER), gspec(SSD_GROUP_W), gspec(SSD_STATE), gspec(SSD_STATE),
            _row_spec(tl, LANE),
        ],
        out_shape=[
            jax.ShapeDtypeStruct((T, SSD_INNER), F32),
            jax.ShapeDtypeStruct((SSD_GROUPS, T, SSD_GROUP_W), BF16),
            jax.ShapeDtypeStruct((SSD_GROUPS, T, SSD_STATE), BF16),
            jax.ShapeDtypeStruct((SSD_GROUPS, T, SSD_STATE), BF16),
            jax.ShapeDtypeStruct((T, LANE), F32),
        ],
        compiler_params=_cparams(),
        name="ssd_in",
    )(x, mod, g, w_main, w_dt, conv_w, conv_b, dt_bias)


def _ssd_scan_kernel(bc_ref, seq_ref, first_ref, last_ref,
                     xf_ref, bf_ref, cf_ref, dtf_ref, xb_ref, bb_ref, cb_ref, dtb_ref,
                     alog_ref, h0_ref, yf_ref, yb_ref, hout_ref,
                     hf_s, hb_s, cs_s, ct_s):
    i = pl.program_id(0)
    Q = SSD_CHUNK
    H, GH, P = SSD_HEADS, SSD_GROUP_HEADS, SSD_HEAD_DIM

    @pl.when(first_ref[i] == 1)
    def _():
        hf_s[...] = h0_ref[0]
        hb_s[...] = h0_ref[1]

    @pl.when(i == 0)
    def _():
        cs_s[...] = jnp.zeros_like(cs_s)

    a_neg =-jnp.exp(alog_ref[...])
    rows = lax.broadcasted_iota(I32, (Q, Q), 0)
    cols = lax.broadcasted_iota(I32, (Q, Q), 1)
    masks = (rows >= cols, rows <= cols)
    dirs = ((xf_ref, bf_ref, cf_ref, dtf_ref, yf_ref, hf_s),
            (xb_ref, bb_ref, cb_ref, dtb_ref, yb_ref, hb_s))

    for d in range(2):
        dt = dirs[d][3][...]
        cum = _tri_cumsum(masks[d].astype(BF16), dt * a_neg)
        ct_s[d] = cum.T
        for g in range(SSD_GROUPS):
            c0 = d * H + g * GH
            cs_s[d, g, :, 0:GH] = cum[:, c0:c0 + GH]
            cs_s[d, g, :, GH:2 * GH] = dt[:, c0:c0 + GH]

    def group(g, carry):
        for d in range(2):
            x_ref, b_ref, c_ref, _, y_ref, h_s = dirs[d]
            cs = cs_s[d, g]
            bg = b_ref[g]
            cg = c_ref[g]
            xg = x_ref[g].astype(F32)
            cb = _dot_nt(cg, bg)
            ht = h_s[g]
            yoff = _dot(cg, ht.astype(BF16))
            tot = cs[Q - 1:Q] if d == 0 else cs[0:1]
            ys, xws, decs = [], [], []
            for r in range(GH):
                ccol = cs[:, r:r + 1]
                dtc = cs[:, GH + r:GH + r + 1]
                crow = ct_s[d, pl.ds(d * H + g * GH + r, 1), :]
                dec = jnp.where(masks[d], jnp.exp(ccol - crow), 0.0)
                m = (cb * dec).astype(BF16)
                xdt = xg[:, r * P:(r + 1) * P] * dtc
                yd = _dot(m, xdt.astype(BF16))
                ys.append(yd + jnp.exp(ccol) * yoff[:, r * P:(r + 1) * P])
                totr = tot[:, r:r + 1]
                xws.append((xdt * jnp.exp(totr - ccol)).astype(BF16))
                decs.append(jnp.broadcast_to(jnp.exp(totr), (1, P)))
            y_ref[g] = jnp.concatenate(ys, axis=1).astype(y_ref.dtype)
            xw = jnp.concatenate(xws, axis=1)
            h_s[g] = ht * jnp.concatenate(decs, axis=1) + _dot_tn(bg, xw)
        return carry

    lax.fori_loop(0, SSD_GROUPS, group, 0)

    @pl.when(last_ref[i] == 1)
    def _():
        hout_ref[0] = hf_s[...]
        hout_ref[1] = hb_s[...]


def _scan_tables(lay, blk):
    bc, seq, first, last = [], [], [], []
    s0 = 0
    sid = 0
    for n, L in ((lay.n_ctx, lay.len_ctx), (lay.n_lat, lay.len_lat)):
        nb = L // blk
        for _ in range(n):
            for k in range(nb):
                bc.append(s0 + nb - 1 - k)
                seq.append(sid)
                first.append(int(k == 0))
                last.append(int(k == nb - 1))
            s0 += nb
            sid += 1
    return tuple(jnp.asarray(np.asarray(a, np.int32)) for a in (bc, seq, first, last))


def _ssd_scan(lay, xs, bm, cm, dt, a_log, h0):
    Q = SSD_CHUNK
    T = xs.shape[1]
    tabs = _scan_tables(lay, Q)
    nseq = lay.n_ctx + lay.n_lat
    G, N, W = SSD_GROUPS, SSD_STATE, SSD_GROUP_W
    fwd = lambda w: pl.BlockSpec((G, Q, w), lambda i, bc, sq, fi, la: (0, i, 0))
    bwd = lambda w: pl.BlockSpec((G, Q, w), lambda i, bc, sq, fi, la: (0, bc[i], 0))
    st_spec = pl.BlockSpec((None, 2, G, N, W), lambda i, bc, sq, fi, la: (sq[i], 0, 0, 0, 0))
    grid_spec = pltpu.PrefetchScalarGridSpec(
        num_scalar_prefetch=4,
        grid=(T // Q,),
        in_specs=[
            fwd(W), fwd(N), fwd(N), pl.BlockSpec((Q, LANE), lambda i, bc, sq, fi, la: (i, 0)),
            bwd(W), bwd(N), bwd(N), pl.BlockSpec((Q, LANE), lambda i, bc, sq, fi, la: (bc[i], 0)),
            pl.BlockSpec((1, LANE), lambda i, *_: (0, 0)),
            st_spec,
        ],
        out_specs=[fwd(W), bwd(W), st_spec],
        scratch_shapes=[
            pltpu.VMEM((G, N, W), F32), pltpu.VMEM((G, N, W), F32),
            pltpu.VMEM((2, G, Q, LANE), F32), pltpu.VMEM((2, LANE, Q), F32),
        ],
    )
    return pl.pallas_call(
        _ssd_scan_kernel,
        grid_spec=grid_spec,
        out_shape=[
            jax.ShapeDtypeStruct((G, T, W), F32),
            jax.ShapeDtypeStruct((G, T, W), F32),
            jax.ShapeDtypeStruct((nseq, 2, G, N, W), F32),
        ],
        compiler_params=_cparams(),
        name="ssd_scan",
    )(*tabs, xs, bm, cm, dt, xs, bm, cm, dt, a_log, h0)


def _ssd_out_kernel(yf_ref, yb_ref, xs_ref, z_ref, x_ref, mod_ref, d_ref, ng_ref, w_ref, o_ref):
    parts = []
    for g in range(SSD_GROUPS):
        dg = d_ref[:, g * SSD_GROUP_W:(g + 1) * SSD_GROUP_W]
        parts.append(yf_ref[g].astype(F32) + yb_ref[g].astype(F32) + dg * xs_ref[g].astype(F32))
    y = jnp.concatenate(parts, axis=1)
    y = _rms(y * _silu(z_ref[...].astype(F32)), ng_ref[...]).astype(BF16)
    o_ref[...] = x_ref[...] + mod_ref[2:3] * _dot(y, w_ref[...])


def _ssd_out(tl, yf, yb, xs, z, x, mod, d_exp, norm_g, w_out):
    gspec = pl.BlockSpec((SSD_GROUPS, tl.tm, SSD_GROUP_W), lambda i: (0, i, 0))
    return pl.pallas_call(
        _ssd_out_kernel,
        grid=(tl.n,),
        in_specs=[gspec, gspec, gspec, _row_spec(tl, SSD_INNER), _row_spec(tl, D_MODEL),
                  _mod_spec(tl), _full(d_exp.shape), _full(norm_g.shape), _full(w_out.shape)],
        out_specs=_row_spec(tl, D_MODEL),
        out_shape=jax.ShapeDtypeStruct((tl.T, D_MODEL), F32),
        compiler_params=_cparams(),
        name="ssd_out",
    )(yf, yb, xs, z, x, mod, d_exp, norm_g, w_out)


def _ssd_layer(tl, lay, x, mod, g, h0, w_in, conv_w, conv_b, a_log, dt_bias, d_skip, norm_g, w_out):
    nseq = h0.shape[0]
    split = SSD_INNER + SSD_CONV_CH
    w_main = w_in[:, :split].astype(BF16)
    w_dt = jnp.pad(w_in[:, split:], ((0, 0), (0, LANE - 2 * SSD_HEADS))).astype(BF16)
    dtb = jnp.pad(dt_bias.reshape(1, -1), ((0, 0), (0, LANE - 2 * SSD_HEADS)))
    alog = jnp.pad(a_log.reshape(1, -1), ((0, 0), (0, LANE - 2 * SSD_HEADS)))
    z, xs, bm, cm, dt = _ssd_in(tl, lay, x, mod, g.reshape(1, -1), w_main, w_dt,
                                conv_w, conv_b.reshape(1, -1), dtb)
    h0t = h0.reshape(nseq, 2, SSD_GROUPS, SSD_GROUP_W, SSD_STATE).swapaxes(-1, -2)
    yf, yb, hfin = _ssd_scan(lay, xs, bm, cm, dt, alog, h0t)
    hfin = hfin.swapaxes(-1, -2).reshape(nseq, 2, SSD_HEADS, SSD_HEAD_DIM, SSD_STATE)
    d_exp = jnp.repeat(d_skip, SSD_HEAD_DIM).reshape(1, -1)
    x = _ssd_out(tl, yf, yb, xs, z, x, mod, d_exp, norm_g.reshape(1, -1), w_out.astype(BF16))
    return x, hfin


def _gla_in_kernel(x_ref, mod_ref, g_ref, w_ref, wgd_ref, wup_ref, bg_ref,
                   q_ref, k_ref, v_ref, r_ref, la_ref):
    mod = mod_ref[...]
    u = (_rms(x_ref[...], g_ref[...]) * (1.0 + mod[1:2]) + mod[0:1]).astype(BF16)
    q_ref[...] = (_dot(u, w_ref[:, :GLA_KEY]) * (GLA_DK ** -0.5)).astype(q_ref.dtype)
    k_ref[...] = _dot(u, w_ref[:, GLA_KEY:2 * GLA_KEY]).astype(k_ref.dtype)
    v_ref[...] = _dot(u, w_ref[:, 2 * GLA_KEY:2 * GLA_KEY + GLA_VAL]).astype(v_ref.dtype)
    r_ref[...] = _dot(u, w_ref[:, 2 * GLA_KEY + GLA_VAL:]).astype(r_ref.dtype)
    gd = _dot(u, wgd_ref[...]).astype(BF16)
    logit = _dot(gd, wup_ref[...]) + bg_ref[...]
    la_ref[...] = (jnp.minimum(logit, 0.0) - jnp.log(1.0 + jnp.exp(-jnp.abs(logit)))) / GLA_TAU


def _gla_in(tl, x, mod, g, w_main, w_gd, w_up, b_gate):
    T = tl.T
    return pl.pallas_call(
        _gla_in_kernel,
        grid=(tl.n,),
        in_specs=[_row_spec(tl, D_MODEL), _mod_spec(tl), _full((1, D_MODEL)), _full(w_main.shape),
                  _full(w_gd.shape), _full(w_up.shape), _full(b_gate.shape)],
        out_specs=[_row_spec(tl, GLA_KEY), _row_spec(tl, GLA_KEY), _row_spec(tl, GLA_VAL),
                   _row_spec(tl, GLA_VAL), _row_spec(tl, 2 * GLA_KEY)],
        out_shape=[
            jax.ShapeDtypeStruct((T, GLA_KEY), BF16),
            jax.ShapeDtypeStruct((T, GLA_KEY), BF16),
            jax.ShapeDtypeStruct((T, GLA_VAL), BF16),
            jax.ShapeDtypeStruct((T, GLA_VAL), F32),
            jax.ShapeDtypeStruct((T, 2 * GLA_KEY), F32),
        ],
        compiler_params=_cparams(),
        name="gla_in",
    )(x, mod, g, w_main, w_gd, w_up, b_gate)


def _gla_scan_kernel(bc_ref, seq_ref, first_ref, last_ref,
                     qf_ref, kf_ref, vf_ref, laf_ref, qb_ref, kb_ref, vb_ref, lab_ref, s0_ref,
                     of_ref, ob_ref, sout_ref, s_s):
    i = pl.program_id(0)
    C = GLA_CHUNK
    nsub = qf_ref.shape[0] // C

    @pl.when(first_ref[i] == 1)
    def _():
        s_s[...] = s0_ref[...]

    rows = lax.broadcasted_iota(I32, (C, C), 0)
    cols = lax.broadcasted_iota(I32, (C, C), 1)
    masks = (rows >= cols, rows <= cols)
    dirs = ((qf_ref, kf_ref, vf_ref, laf_ref, of_ref), (qb_ref, kb_ref, vb_ref, lab_ref, ob_ref))

    def sub(c, carry):
        for d in range(2):
            q_ref, k_ref, v_ref, la_ref, o_ref = dirs[d]
            cc = c if d == 0 else nsub - 1 - c
            rs = pl.ds(pl.multiple_of(cc * C, C), C)
            tri = masks[d].astype(BF16)
            for h in range(GLA_HEADS):
                ks = slice(h * GLA_DK, (h + 1) * GLA_DK)
                vs = slice(h * GLA_DV, (h + 1) * GLA_DV)
                g = la_ref[rs, d * GLA_KEY + h * GLA_DK:d * GLA_KEY + (h + 1) * GLA_DK]
                gcum = _tri_cumsum(tri, g)
                qh = q_ref[rs, ks].astype(F32)
                kh = k_ref[rs, ks].astype(F32)
                vh = v_ref[rs, vs]
                qi = (qh * jnp.exp(gcum)).astype(BF16)
                ki = (kh * jnp.exp(-gcum)).astype(BF16)
                att = jnp.where(masks[d], _dot_nt(qi, ki), 0.0).astype(BF16)
                st = s_s[d, h]
                o_ref[rs, vs] = (_dot(att, vh) + _dot_nt(qi, st.astype(BF16))).astype(o_ref.dtype)
                glast = gcum[C - 1:C] if d == 0 else gcum[0:1]
                kdec = (kh * jnp.exp(glast - gcum)).astype(BF16)
                s_s[d, h] = st * jnp.exp(glast) + _dot_tn(vh, kdec)
        return carry

    lax.fori_loop(0, nsub, sub, 0)

    @pl.when(last_ref[i] == 1)
    def _():
        sout_ref[...] = s_s[...]


def _gla_scan(lay, q, k, v, la, s0):
    B = SCAN_BLK
    T = q.shape[0]
    tabs = _scan_tables(lay, B)
    nseq = lay.n_ctx + lay.n_lat
    fwd = lambda w: pl.BlockSpec((B, w), lambda i, bc, sq, fi, la_: (i, 0))
    bwd = lambda w: pl.BlockSpec((B, w), lambda i, bc, sq, fi, la_: (bc[i], 0))
    st_spec = pl.BlockSpec((None, 2, GLA_HEADS, GLA_DV, GLA_DK),
                           lambda i, bc, sq, fi, la_: (sq[i], 0, 0, 0, 0))
    grid_spec = pltpu.PrefetchScalarGridSpec(
        num_scalar_prefetch=4,
        grid=(T // B,),
        in_specs=[fwd(GLA_KEY), fwd(GLA_KEY), fwd(GLA_VAL), fwd(2 * GLA_KEY),
                  bwd(GLA_KEY), bwd(GLA_KEY), bwd(GLA_VAL), bwd(2 * GLA_KEY), st_spec],
        out_specs=[fwd(GLA_VAL), bwd(GLA_VAL), st_spec],
        scratch_shapes=[pltpu.VMEM((2, GLA_HEADS, GLA_DV, GLA_DK), F32)],
    )
    return pl.pallas_call(
        _gla_scan_kernel,
        grid_spec=grid_spec,
        out_shape=[
            jax.ShapeDtypeStruct((T, GLA_VAL), F32),
            jax.ShapeDtypeStruct((T, GLA_VAL), F32),
            jax.ShapeDtypeStruct((nseq, 2, GLA_HEADS, GLA_DV, GLA_DK), F32),
        ],
        compiler_params=_cparams(),
        name="gla_scan",
    )(*tabs, q, k, v, la, q, k, v, la, s0)


def _gla_out_kernel(of_ref, ob_ref, r_ref, x_ref, mod_ref, ng_ref, w_ref, o_ref):
    o = of_ref[...].astype(F32) + ob_ref[...].astype(F32)
    ng = ng_ref[...]
    parts = [_rms(o[:, h * GLA_DV:(h + 1) * GLA_DV], ng) for h in range(GLA_HEADS)]
    y = (jnp.concatenate(parts, axis=1) * _silu(r_ref[...].astype(F32))).astype(BF16)
    o_ref[...] = x_ref[...] + mod_ref[2:3] * _dot(y, w_ref[...])


def _gla_out(tl, of, ob, r, x, mod, norm_g, w_out):
    return pl.pallas_call(
        _gla_out_kernel,
        grid=(tl.n,),
        in_specs=[_row_spec(tl, GLA_VAL), _row_spec(tl, GLA_VAL), _row_spec(tl, GLA_VAL),
                  _row_spec(tl, D_MODEL), _mod_spec(tl), _full(norm_g.shape), _full(w_out.shape)],
        out_specs=_row_spec(tl, D_MODEL),
        out_shape=jax.ShapeDtypeStruct((tl.T, D_MODEL), F32),
        compiler_params=_cparams(),
        name="gla_out",
    )(of, ob, r, x, mod, norm_g, w_out)


def _gla_layer(tl, lay, x, mod, g, s0, w_in, w_gate_up, b_gate, norm_g, w_out):
    split = 2 * GLA_KEY + 2 * GLA_VAL
    w_main = w_in[:, :split].astype(BF16)
    w_gd = jnp.pad(w_in[:, split:], ((0, 0), (0, LANE - 2 * GLA_RANK))).astype(BF16)
    w_up = jnp.zeros((LANE, 2 * GLA_KEY), F32)
    for d in range(2):
        w_up = w_up.at[d * GLA_RANK:(d + 1) * GLA_RANK, d * GLA_KEY:(d + 1) * GLA_KEY].set(w_gate_up[d])
    q, k, v, r, la = _gla_in(tl, x, mod, g.reshape(1, -1), w_main, w_gd, w_up.astype(BF16),
                             b_gate.reshape(1, -1))
    of, ob, sfin = _gla_scan(lay, q, k, v, la, s0.swapaxes(-1, -2))
    x = _gla_out(tl, of, ob, r, x, mod, norm_g.reshape(1, -1), w_out.astype(BF16))
    return x, sfin.swapaxes(-1, -2)


def _gelu_tanh(x):
    return 0.5 * x * (1.0 + jnp.tanh(0.7978845608028654 * (x + 0.044715 * (x * x * x))))


def _lru_in_kernel(x_ref, mod_ref, g_ref, w_ref, cw_ref, cb_ref, wg_ref, bg_ref, lam_ref,
                   a_ref, b_ref, gg_ref, *, nct, seg_ctx, seg_lat):
    i = pl.program_id(0)
    tm = x_ref.shape[0]
    W = LRU_WIDTH
    mod = mod_ref[...]
    u = (_rms(x_ref[...], g_ref[...]) * (1.0 + mod[1:2]) + mod[0:1]).astype(BF16)
    gg_ref[...] = _gelu_tanh(_dot(u, w_ref[:, :W])).astype(gg_ref.dtype)
    masks = _conv_masks(tm, jnp.where(i < nct, seg_ctx, seg_lat))
    xb = _conv4(_dot(u, w_ref[:, W:]), cw_ref[...], cb_ref[...], masks)
    sp = _softplus(-lam_ref[...])
    for n in range(LRU_BLOCKS):
        cs = slice(n * LRU_BLOCK, (n + 1) * LRU_BLOCK)
        xn = xb[:, cs]
        gates = jax.nn.sigmoid(_dot(xn.astype(BF16), wg_ref[n]) + bg_ref[n])
        for d in range(2):
            r = gates[:, d * LRU_BLOCK:(d + 1) * LRU_BLOCK]
            ig = gates[:, (2 + d) * LRU_BLOCK:(3 + d) * LRU_BLOCK]
            log_a = (-LRU_C) * r * sp[:, d * W + n * LRU_BLOCK:d * W + (n + 1) * LRU_BLOCK]
            a = jnp.exp(log_a)
            ds = slice(d * W + n * LRU_BLOCK, d * W + (n + 1) * LRU_BLOCK)
            a_ref[:, ds] = a
            b_ref[:, ds] = jnp.sqrt(1.0 - a * a) * (ig * xn)


def _lru_in(tl, lay, x, mod, g, w_in, conv_w, conv_b, w_gates, b_gates, lam):
    T = tl.T
    kern = functools.partial(_lru_in_kernel, nct=tl.nct, seg_ctx=lay.len_ctx, seg_lat=GRID_W)
    return pl.pallas_call(
        kern,
        grid=(tl.n,),
        in_specs=[_row_spec(tl, D_MODEL), _mod_spec(tl), _full((1, D_MODEL)), _full(w_in.shape),
                  _full(conv_w.shape), _full(conv_b.shape), _full(w_gates.shape),
                  _full(b_gates.shape), _full(lam.shape)],
        out_specs=[_row_spec(tl, 2 * LRU_WIDTH), _row_spec(tl, 2 * LRU_WIDTH), _row_spec(tl, LRU_WIDTH)],
        out_shape=[
            jax.ShapeDtypeStruct((T, 2 * LRU_WIDTH), F32),
            jax.ShapeDtypeStruct((T, 2 * LRU_WIDTH), F32),
            jax.ShapeDtypeStruct((T, LRU_WIDTH), F32),
        ],
        compiler_params=_cparams(),
        name="lru_in",
    )(x, mod, g, w_in, conv_w, conv_b, w_gates, b_gates, lam)


def _lru_scan_kernel(bc_ref, seq_ref, first_ref, last_ref,
                     af_ref, bf_ref, ab_ref, bb_ref, h0_ref, hf_ref, hb_ref, hout_ref, h_s):
    i = pl.program_id(0)
    S = SUBLANE
    W = LRU_WIDTH
    ntile = af_ref.shape[0] // S

    @pl.when(first_ref[i] == 1)
    def _():
        h_s[...] = h0_ref[...]

    ri = lax.broadcasted_iota(I32, (S, W), 0)

    def tile(j, carry):
        hf, hb = carry
        rs = pl.ds(pl.multiple_of(j * S, S), S)
        a = af_ref[rs, 0:W]
        b = bf_ref[rs, 0:W]
        for sh in (1, 2, 4):
            keep = ri >= sh
            a_s = jnp.where(keep, pltpu.roll(a, sh, 0), 1.0)
            b_s = jnp.where(keep, pltpu.roll(b, sh, 0), 0.0)
            b = a * b_s + b
            a = a * a_s
        h8 = a * hf + b
        hf_ref[rs, :] = h8
        hf = h8[S - 1:S]
        rs = pl.ds(pl.multiple_of((ntile - 1 - j) * S, S), S)
        a = ab_ref[rs, W:2 * W]
        b = bb_ref[rs, W:2 * W]
        for sh in (1, 2, 4):
            keep = ri < S - sh
            a_s = jnp.where(keep, pltpu.roll(a, S - sh, 0), 1.0)
            b_s = jnp.where(keep, pltpu.roll(b, S - sh, 0), 0.0)
            b = a * b_s + b
            a = a * a_s
        h8 = a * hb + b
        hb_ref[rs, :] = h8
        hb = h8[0:1]
        return hf, hb

    hf, hb = lax.fori_loop(0, ntile, tile, (h_s[0:1], h_s[1:2]))
    h_s[0:1] = hf
    h_s[1:2] = hb

    @pl.when(last_ref[i] == 1)
    def _():
        hout_ref[...] = h_s[...]


def _lru_scan(lay, a, b, h0):
    B = SCAN_BLK
    T = a.shape[0]
    W = LRU_WIDTH
    tabs = _scan_tables(lay, B)
    nseq = lay.n_ctx + lay.n_lat
    fwd = lambda w: pl.BlockSpec((B, w), lambda i, bc, sq, fi, la_: (i, 0))
    bwd = lambda w: pl.BlockSpec((B, w), lambda i, bc, sq, fi, la_: (bc[i], 0))
    st_spec = pl.BlockSpec((None, 2, W), lambda i, bc, sq, fi, la_: (sq[i], 0, 0))
    grid_spec = pltpu.PrefetchScalarGridSpec(
        num_scalar_prefetch=4,
        grid=(T // B,),
        in_specs=[fwd(2 * W), fwd(2 * W), bwd(2 * W), bwd(2 * W), st_spec],
        out_specs=[fwd(W), bwd(W), st_spec],
        scratch_shapes=[pltpu.VMEM((2, W), F32)],
    )
    return pl.pallas_call(
        _lru_scan_kernel,
        grid_spec=grid_spec,
        out_shape=[
            jax.ShapeDtypeStruct((T, W), F32),
            jax.ShapeDtypeStruct((T, W), F32),
            jax.ShapeDtypeStruct((nseq, 2, W), F32),
        ],
        compiler_params=_cparams(),
        name="lru_scan",
    )(*tabs, a, b, a, b, h0)


def _lru_out_kernel(hf_ref, hb_ref, gg_ref, x_ref, mod_ref, w_ref, o_ref):
    y = ((hf_ref[...] + hb_ref[...]) * gg_ref[...].astype(F32)).astype(BF16)
    o_ref[...] = x_ref[...] + mod_ref[2:3] * _dot(y, w_ref[...])


def _lru_out(tl, hf, hb, gg, x, mod, w_out):
    W = LRU_WIDTH
    return pl.pallas_call(
        _lru_out_kernel,
        grid=(tl.n,),
        in_specs=[_row_spec(tl, W), _row_spec(tl, W), _row_spec(tl, W), _row_spec(tl, D_MODEL),
                  _mod_spec(tl), _full(w_out.shape)],
        out_specs=_row_spec(tl, D_MODEL),
        out_shape=jax.ShapeDtypeStruct((tl.T, D_MODEL), F32),
        compiler_params=_cparams(),
        name="lru_out",
    )(hf, hb, gg, x, mod, w_out)


def _lru_layer(tl, lay, x, mod, g, h0, w_in, conv_w, conv_b, w_a, b_a, w_x, b_x, lam, w_out):
    w_gates = jnp.concatenate([w_a[0], w_a[1], w_x[0], w_x[1]], axis=-1).astype(BF16)
    blk = lambda v: v.reshape(2, LRU_BLOCKS, LRU_BLOCK)
    b_gates = jnp.concatenate([blk(b_a)[0], blk(b_a)[1], blk(b_x)[0], blk(b_x)[1]], axis=-1)
    b_gates = b_gates.reshape(LRU_BLOCKS, 1, 4 * LRU_BLOCK)
    a, b, gg = _lru_in(tl, lay, x, mod, g.reshape(1, -1), w_in.astype(BF16), conv_w,
                       conv_b.reshape(1, -1), w_gates, b_gates, lam.reshape(1, -1))
    hf, hb, hfin = _lru_scan(lay, a, b, h0)
    x = _lru_out(tl, hf, hb, gg, x, mod, w_out.astype(BF16))
    return x, hfin


def _swiglu(u, w1_ref, w3_ref, w2_ref):
    a = _dot(u, w1_ref[...])
    b = _dot(u, w3_ref[...])
    return _dot((_silu(a) * b).astype(BF16), w2_ref[...])


def _ffn_kernel(x_ref, mod_ref, g_ref, w1_ref, w3_ref, w2_ref, o_ref):
    x = x_ref[...]
    mod = mod_ref[...]
    u = (_rms(x, g_ref[...]) * (1.0 + mod[4:5]) + mod[3:4]).astype(BF16)
    o_ref[...] = x + mod[5:6] * _swiglu(u, w1_ref, w3_ref, w2_ref)


def _ffn_layer(tl, x, mod, g, w1, w3, w2):
    w1, w3, w2 = w1.astype(BF16), w3.astype(BF16), w2.astype(BF16)
    return pl.pallas_call(
        _ffn_kernel,
        grid=(tl.n,),
        in_specs=[_row_spec(tl, D_MODEL), _mod_spec(tl), _full((1, D_MODEL)),
                  _full(w1.shape), _full(w3.shape), _full(w2.shape)],
        out_specs=_row_spec(tl, D_MODEL),
        out_shape=jax.ShapeDtypeStruct((tl.T, D_MODEL), F32),
        compiler_params=_cparams(),
        name="ffn",
    )(x, mod, g.reshape(1, -1), w1, w3, w2)


SLAB = D_MODEL // LANE
assert SLAB == SUBLANE


def _to_slab(ref, val):
    tm = val.shape[0]
    for s in range(SLAB):
        ref[pl.ds(s, tm, stride=SLAB), :] = val[:, s * LANE:(s + 1) * LANE]


def _from_slab(ref, tm):
    return jnp.concatenate([ref[pl.ds(s, tm, stride=SLAB), :] for s in range(SLAB)], axis=1)


def _moe_route_kernel(x_ref, mod_ref, g_ref, wr_ref, u_ref, mi_ref, mf_ref, cnt_ref, cnt_s):
    i = pl.program_id(0)
    tm = x_ref.shape[0]

    @pl.when(i == 0)
    def _():
        cnt_s[...] = jnp.zeros_like(cnt_s)

    mod = mod_ref[...]
    u = _rms(x_ref[...], g_ref[...]) * (1.0 + mod[4:5]) + mod[3:4]
    _to_slab(u_ref, u)
    logits = jnp.dot(u, wr_ref[...], preferred_element_type=F32, precision=lax.Precision.HIGHEST)
    lane = lax.broadcasted_iota(I32, (tm, LANE), 1)
    lg = jnp.where(lane < N_EXPERTS, logits, -jnp.inf)
    m1 = jnp.max(lg, axis=1, keepdims=True)
    i1 = jnp.min(jnp.where(lg == m1, lane, LANE), axis=1, keepdims=True)
    lg2 = jnp.where(lane == i1, -jnp.inf, lg)
    m2 = jnp.max(lg2, axis=1, keepdims=True)
    i2 = jnp.min(jnp.where(lg2 == m2, lane, LANE), axis=1, keepdims=True)
    e2 = jnp.exp(m2 - m1)
    g1 = 1.0 / (1.0 + e2)
    g2 = e2 / (1.0 + e2)
    oh1 = lane == i1
    oh2 = lane == i2
    cnt = (oh1 | oh2).astype(BF16)
    rr = lax.broadcasted_iota(I32, (tm, tm), 0)
    cc = lax.broadcasted_iota(I32, (tm, tm), 1)
    before = _dot((rr > cc).astype(BF16), cnt) + cnt_s[...]
    r1 = jnp.sum(jnp.where(oh1, before, 0.0), axis=1, keepdims=True).astype(I32)
    r2 = jnp.sum(jnp.where(oh2, before, 0.0), axis=1, keepdims=True).astype(I32)
    cnt_s[...] = cnt_s[...] + jnp.sum(cnt.astype(F32), axis=0, keepdims=True)
    mi_ref[...] = jnp.where(lane == 0, i1, jnp.where(lane == 1, i2, jnp.where(lane == 2, r1, r2)))
    mf_ref[...] = jnp.where(lane == 0, g1, g2)
    cnt_ref[...] = cnt_s[...]


def _moe_route(tl, x, mod, g, w_router):
    T = tl.T
    return pl.pallas_call(
        _moe_route_kernel,
        grid=(tl.n,),
        in_specs=[_row_spec(tl, D_MODEL), _mod_spec(tl), _full((1, D_MODEL)), _full(w_router.shape)],
        out_specs=[pl.BlockSpec((tl.tm * SLAB, LANE), lambda i: (i, 0)),
                   _row_spec(tl, LANE), _row_spec(tl, LANE), _full((1, LANE))],
        out_shape=[
            jax.ShapeDtypeStruct((T * SLAB, LANE), F32),
            jax.ShapeDtypeStruct((T, LANE), I32),
            jax.ShapeDtypeStruct((T, LANE), F32),
            jax.ShapeDtypeStruct((1, LANE), F32),
        ],
        scratch_shapes=[pltpu.VMEM((1, LANE), F32)],
        compiler_params=_cparams(),
        name="moe_route",
    )(x, mod, g, w_router)


DISPATCH_BLK = 512


def _moe_dispatch_kernel(pos_ref, tail_ref, u_hbm, z_hbm, xs_hbm, sem, *, nblk, tm, nt):
    i = pl.program_id(0)
    base = i * nblk

    def issue(r, carry):
        for k in range(2):
            pltpu.make_async_copy(u_hbm.at[base + r], xs_hbm.at[pos_ref[2 * r + k]], sem).start()
        return carry

    lax.fori_loop(0, nblk, issue, 0)
    pltpu.make_async_copy(u_hbm.at[pl.ds(0, 2 * nblk)], xs_hbm.at[pl.ds(0, 2 * nblk)], sem).wait()

    @pl.when(i == pl.num_programs(0) - 1)
    def _():
        for e in range(N_EXPERTS):
            start = tail_ref[e]
            n = tail_ref[N_EXPERTS + e]

            def fill(r, carry):
                pltpu.make_async_copy(z_hbm.at[0], xs_hbm.at[start + r], sem).start()
                return carry

            def drain(r, carry):
                pltpu.make_async_copy(z_hbm.at[0], xs_hbm.at[start + r], sem).wait()
                return carry

            lax.fori_loop(0, n, fill, 0)
            lax.fori_loop(0, n, drain, 0)

        def fill_tile(j, carry):
            dst = xs_hbm.at[pl.ds(pl.multiple_of(j * tm, tm), tm)]
            pltpu.make_async_copy(z_hbm, dst, sem).start()
            pltpu.make_async_copy(z_hbm, dst, sem).wait()
            return carry

        lax.fori_loop(tail_ref[2 * N_EXPERTS], nt, fill_tile, 0)


def _moe_dispatch(T, P, tm, pos, tail, u_slab):
    nblk = DISPATCH_BLK
    assert T % nblk == 0 and P % tm == 0
    kern = functools.partial(_moe_dispatch_kernel, nblk=nblk, tm=tm, nt=P // tm)
    zrow = jnp.zeros((tm, SLAB, LANE), F32)
    xs = pl.pallas_call(
        kern,
        grid=(T // nblk,),
        in_specs=[
            pl.BlockSpec((2 * nblk,), lambda i: (i,), memory_space=pltpu.SMEM),
            pl.BlockSpec(memory_space=pltpu.SMEM),
            pl.BlockSpec(memory_space=pl.ANY),
            pl.BlockSpec(memory_space=pl.ANY),
        ],
        out_specs=pl.BlockSpec(memory_space=pl.ANY),
        out_shape=jax.ShapeDtypeStruct((P, SLAB, LANE), F32),
        scratch_shapes=[pltpu.SemaphoreType.DMA(())],
        compiler_params=_cparams(),
        name="moe_dispatch",
    )(pos, tail, u_slab.reshape(T, SLAB, LANE), zrow)
    return xs


def _moe_ffn_kernel(texp_ref, nv_ref, xs_ref, w1_ref, w3_ref, w2_ref, ys_ref):
    j = pl.program_id(0)
    tm = xs_ref.shape[0] // SLAB
    valid = j < nv_ref[0]

    @pl.when(valid)
    def _():
        u = _from_slab(xs_ref, tm).astype(BF16)
        _to_slab(ys_ref, _swiglu(u, w1_ref, w3_ref, w2_ref))

    @pl.when(jnp.logical_not(valid))
    def _():
        ys_ref[...] = jnp.zeros_like(ys_ref)


def _moe_ffn(tm, nt, texp, nv, xs2d, w1, w3, w2):
    wspec = lambda shp: pl.BlockSpec((None,) + shp, lambda j, te, nv_: (te[j], 0, 0))
    grid_spec = pltpu.PrefetchScalarGridSpec(
        num_scalar_prefetch=2,
        grid=(nt,),
        in_specs=[
            pl.BlockSpec((tm * SLAB, LANE), lambda j, te, nv_: (jnp.minimum(j, nv_[0] - 1), 0)),
            wspec(w1.shape[1:]), wspec(w3.shape[1:]), wspec(w2.shape[1:]),
        ],
        out_specs=pl.BlockSpec((tm * SLAB, LANE), lambda j, te, nv_: (j, 0)),
    )
    return pl.pallas_call(
        _moe_ffn_kernel,
        grid_spec=grid_spec,
        out_shape=jax.ShapeDtypeStruct(xs2d.shape, F32),
        compiler_params=_cparams(),
        name="moe_ffn",
    )(texp, nv, xs2d, w1, w3, w2)


def _moe_combine_kernel(pos_ref, x_ref, mod_ref, mf_ref, ys_hbm, o_ref, buf0, buf1, sem):
    tm = x_ref.shape[0]
    bufs = (buf0, buf1)

    def issue(r, carry):
        for k in range(2):
            dst = bufs[k].at[pl.ds(pl.multiple_of(r * SLAB, SLAB), SLAB)]
            pltpu.make_async_copy(ys_hbm.at[pos_ref[2 * r + k]], dst, sem).start()
        return carry

    lax.fori_loop(0, tm, issue, 0)
    for k in range(2):
        pltpu.make_async_copy(bufs[1 - k], bufs[k], sem).wait()
    mf = mf_ref[...]
    g1 = mf[:, 0:1]
    g2 = mf[:, 1:2]
    for s in range(SLAB):
        cs = slice(s * LANE, (s + 1) * LANE)
        y = g1 * buf0[pl.ds(s, tm, stride=SLAB), :] + g2 * buf1[pl.ds(s, tm, stride=SLAB), :]
        o_ref[:, cs] = x_ref[:, cs] + mod_ref[5:6, cs] * y


def _moe_combine(tl, pos, x, mod, mf, ys):
    tm = tl.tm
    return pl.pallas_call(
        _moe_combine_kernel,
        grid=(tl.n,),
        in_specs=[
            pl.BlockSpec((2 * tm,), lambda i: (i,), memory_space=pltpu.SMEM),
            _row_spec(tl, D_MODEL), _mod_spec(tl), _row_spec(tl, LANE),
            pl.BlockSpec(memory_space=pl.ANY),
        ],
        out_specs=_row_spec(tl, D_MODEL),
        out_shape=jax.ShapeDtypeStruct((tl.T, D_MODEL), F32),
        scratch_shapes=[pltpu.VMEM((tm * SLAB, LANE), F32), pltpu.VMEM((tm * SLAB, LANE), F32),
                        pltpu.SemaphoreType.DMA(())],
        compiler_params=_cparams(),
        name="moe_combine",
    )(pos, x, mod, mf, ys)


def _moe_layer(tl, x, mod, g, w_router, w1, w3, w2):
    T, tm = tl.T, tl.tm
    nt = 2 * tl.n + N_EXPERTS
    P = nt * tm
    wr = jnp.pad(w_router, ((0, 0), (0, LANE - N_EXPERTS)))
    u_slab, mi, mf, cnt = _moe_route(tl, x, mod, g.reshape(1, -1), wr)
    counts = cnt[0, :N_EXPERTS].astype(I32)
    tiles_e = (counts + tm - 1) // tm
    cum_tiles = jnp.cumsum(tiles_e)
    nv = cum_tiles[-1:]
    off = (cum_tiles - tiles_e) * tm
    sel = mi[:, 0:2, None] == jnp.arange(N_EXPERTS, dtype=I32)
    pos = (jnp.sum(jnp.where(sel, off, 0), axis=-1) + mi[:, 2:4]).reshape(2 * T)
    jj = jnp.minimum(jnp.arange(nt, dtype=I32), nv[0] - 1)
    texp = jnp.sum((jj[:, None] >= cum_tiles[None, :]).astype(I32), axis=1)
    tail = jnp.concatenate([off + counts, tiles_e * tm - counts, nv])
    xs = _moe_dispatch(T, P, tm, pos, tail, u_slab)
    ys = _moe_ffn(tm, nt, texp, nv, xs.reshape(P * SLAB, LANE),
                  w1.astype(BF16), w3.astype(BF16), w2.astype(BF16))
    return _moe_combine(tl, pos, x, mod, mf, ys.reshape(P, SLAB, LANE))


def _final_kernel(x_ref, g_ref, o_ref):
    o_ref[...] = _rms(x_ref[...], g_ref[...])


def _final_norm(tl, x, g):
    return pl.pallas_call(
        _final_kernel,
        grid=(tl.n,),
        in_specs=[_row_spec(tl, D_MODEL), _full((1, D_MODEL))],
        out_specs=_row_spec(tl, D_MODEL),
        out_shape=jax.ShapeDtypeStruct((tl.T, D_MODEL), F32),
        compiler_params=_cparams(),
        name="final_norm",
    )(x, g.reshape(1, -1))


def kernel(x_prompt, x_sample, state_ssd, state_gla, state_lru, c, c_ctx, ada_w, ada_b, norm_g, final_g, ssd_w_in, ssd_conv_w, ssd_conv_b, ssd_a_log, ssd_dt_bias, ssd_d, ssd_norm_g, ssd_w_out, gla_w_in, gla_w_gate_up, gla_b_gate, gla_norm_g, gla_w_out, lru_w_in, lru_conv_w, lru_conv_b, lru_w_a, lru_b_a, lru_w_x, lru_b_x, lru_lambda, lru_w_out, ffn_w1, ffn_w3, ffn_w2, moe_router, moe_w1, moe_w3, moe_w2):
    n_ctx, len_ctx, _ = x_prompt.shape
    n_lat, len_lat, _ = x_sample.shape
    lay = Layout(n_ctx, len_ctx, n_lat, len_lat)
    tl = _Tiles(lay, TM)
    assert len_ctx & (len_ctx - 1) == 0 and len_ctx <= TM and TM % GRID_W == 0

    x = jnp.concatenate([x_prompt.reshape(-1, D_MODEL), x_sample.reshape(-1, D_MODEL)], axis=0)
    c_all = jnp.concatenate([c_ctx[None], c], axis=0)
    c_all = jnp.pad(c_all, ((0, -c_all.shape[0] % SUBLANE), (0, 0)))
    mods = _mods(c_all, ada_w, ada_b)

    caches = (state_ssd, state_gla, state_lru)
    ctx_states = ([], [], [])
    for i in range(DEPTH):
        kind, j = i % N_MIXERS, i // N_MIXERS
        cache = caches[kind][:, j]
        h0 = jnp.concatenate([jnp.zeros((n_ctx,) + cache.shape[1:], F32), cache], axis=0)
        if kind == 0:
            x, hfin = _ssd_layer(tl, lay, x, mods[i], norm_g[i, 0], h0, ssd_w_in[j], ssd_conv_w[j],
                                 ssd_conv_b[j], ssd_a_log[j], ssd_dt_bias[j], ssd_d[j],
                                 ssd_norm_g[j], ssd_w_out[j])
        elif kind == 1:
            x, hfin = _gla_layer(tl, lay, x, mods[i], norm_g[i, 0], h0, gla_w_in[j], gla_w_gate_up[j],
                                 gla_b_gate[j], gla_norm_g[j], gla_w_out[j])
        else:
            x, hfin = _lru_layer(tl, lay, x, mods[i], norm_g[i, 0], h0, lru_w_in[j], lru_conv_w[j],
                                 lru_conv_b[j], lru_w_a[j], lru_b_a[j], lru_w_x[j], lru_b_x[j],
                                 lru_lambda[j], lru_w_out[j])
        ctx_states[kind].append(hfin[:n_ctx])
        if i % 2 == 0:
            x = _ffn_layer(tl, x, mods[i], norm_g[i, 1], ffn_w1[i // 2], ffn_w3[i // 2], ffn_w2[i // 2])
        else:
            x = _moe_layer(tl, x, mods[i], norm_g[i, 1], moe_router[i // 2], moe_w1[i // 2],
                           moe_w3[i // 2], moe_w2[i // 2])

    y = _final_norm(tl, x, final_g)
    n_c = n_ctx * len_ctx
    return (y[:n_c].reshape(n_ctx, len_ctx, D_MODEL),
            y[n_c:].reshape(n_lat, len_lat, D_MODEL),
            jnp.stack(ctx_states[0], axis=1),
            jnp.stack(ctx_states[1], axis=1),
            jnp.stack(ctx_states[2], axis=1))
```

```python
import collections
import functools

import jax
import jax.numpy as jnp
import numpy as np
from jax import lax
from jax.experimental import pallas as pl
from jax.experimental.pallas import tpu as pltpu

F32 = jnp.float32
BF16 = jnp.bfloat16
I32 = jnp.int32

D_MODEL = 1024
DEPTH = 4
GRID_W = 64
N_MIXERS = 3
EPS = 1e-6
CONV_W = 4

SSD_INNER = 2 * D_MODEL
SSD_HEAD_DIM = 64
SSD_HEADS = SSD_INNER // SSD_HEAD_DIM
SSD_GROUPS = 8
SSD_GROUP_HEADS = SSD_HEADS // SSD_GROUPS
SSD_GROUP_W = SSD_GROUP_HEADS * SSD_HEAD_DIM
SSD_STATE = 128
SSD_CHUNK = 128
SSD_BC = SSD_GROUPS * SSD_STATE
SSD_CONV_CH = SSD_INNER + 2 * SSD_BC

GLA_HEADS = 4
GLA_KEY = D_MODEL // 2
GLA_VAL = D_MODEL
GLA_DK = GLA_KEY // GLA_HEADS
GLA_DV = GLA_VAL // GLA_HEADS
GLA_RANK = 16
GLA_TAU = 16.0
GLA_CHUNK = 64

LRU_WIDTH = D_MODEL
LRU_BLOCKS = 8
LRU_BLOCK = LRU_WIDTH // LRU_BLOCKS
LRU_C = 8.0

D_FF = 2816
N_EXPERTS = 8

LANE = 128
SUBLANE = 8
TM = 256
SCAN_BLK = 256
VMEM_LIMIT = 56 * 1024 * 1024

Layout = collections.namedtuple("Layout", "n_ctx len_ctx n_lat len_lat")


def _cparams(n_axes=1):
    return pltpu.CompilerParams(
        dimension_semantics=("arbitrary",) * n_axes, vmem_limit_bytes=VMEM_LIMIT)


def _silu(x):
    return x * jax.nn.sigmoid(x)


def _softplus(x):
    return jnp.maximum(x, 0.0) + jnp.log(1.0 + jnp.exp(-jnp.abs(x)))


def _rms(x, g):
    return x * lax.rsqrt(jnp.mean(x * x, axis=-1, keepdims=True) + EPS) * g


def _dot(a, b):
    return jnp.dot(a, b, preferred_element_type=F32)


def _dot_nt(a, b):
    return lax.dot_general(a, b, (((1,), (1,)), ((), ())), preferred_element_type=F32)


def _dot_tn(a, b):
    return lax.dot_general(a, b, (((0,), (0,)), ((), ())), preferred_element_type=F32)


def _split3(x):
    hi = x.astype(BF16)
    r1 = x - hi.astype(F32)
    mid = r1.astype(BF16)
    lo = (r1 - mid.astype(F32)).astype(BF16)
    return hi, mid, lo


def _tri_cumsum(tri, x):
    hi, mid, lo = _split3(x)
    return _dot(tri, hi) + _dot(tri, mid) + _dot(tri, lo)


class _Tiles:
    def __init__(self, lay, tm):
        assert lay.len_ctx % tm == 0 and lay.len_lat % tm == 0
        self.tm = tm
        self.nct = lay.n_ctx * lay.len_ctx // tm
        self.tpl = lay.len_lat // tm
        self.n = self.nct + lay.n_lat * self.tpl
        self.T = self.n * tm

    def mod_row(self, i):
        return jnp.where(i < self.nct, 0, 1 + (i - self.nct) // self.tpl)


def _full(shape):
    nd = len(shape)
    return pl.BlockSpec(shape, lambda *_: (0,) * nd)


def _mod_spec(tl):
    return pl.BlockSpec((None, 6, D_MODEL), lambda i, *_: (tl.mod_row(i), 0, 0))


def _row_spec(tl, width):
    return pl.BlockSpec((tl.tm, width), lambda i, *_: (i, 0))


def _mods_kernel(c_ref, w_ref, b_ref, o_ref):
    c = c_ref[...]
    a = _silu(c)
    o_ref[...] = jnp.dot(a, w_ref[...], preferred_element_type=F32,
                         precision=lax.Precision.HIGHEST) + b_ref[...]


def _mods(c_all, ada_w, ada_b):
    R = c_all.shape[0]
    nb = 6
    out = pl.pallas_call(
        _mods_kernel,
        grid=(DEPTH, nb),
        in_specs=[
            pl.BlockSpec((R, D_MODEL), lambda l, j: (0, 0)),
            pl.BlockSpec((None, D_MODEL, D_MODEL), lambda l, j: (l, 0, j)),
            pl.BlockSpec((None, 1, D_MODEL), lambda l, j: (l, 0, j)),
        ],
        out_specs=pl.BlockSpec((None, R, D_MODEL), lambda l, j: (l, 0, j)),
        out_shape=jax.ShapeDtypeStruct((DEPTH, R, 6 * D_MODEL), F32),
        compiler_params=_cparams(2),
        name="mods",
    )(c_all, ada_w, ada_b.reshape(DEPTH, 1, 6 * D_MODEL))
    return out.reshape(DEPTH, R, 6, D_MODEL)


CONV_OFFSETS = (-1, 0, 1, 2)


def _shift_mats(tm, segs):
    t = np.arange(tm)
    out = []
    for seg in segs:
        mats = []
        for off in CONV_OFFSETS:
            if off == 0:
                continue
            src = t + off
            ok = (src >= 0) & (src < tm) & (src // seg == t // seg)
            m = np.zeros((tm, tm), np.float32)
            m[t[ok], src[ok]] = 1.0
            mats.append(m)
        out.append(np.concatenate(mats, axis=1))
    return jnp.asarray(np.stack(out), BF16)


def _shift_spec(tl):
    return pl.BlockSpec((None, tl.tm, 3 * tl.tm), lambda i: (jnp.where(i < tl.nct, 0, 1), 0, 0))


def _conv4(p, cw, cb, shifts):
    taps = [(p * cw[k:k + 1]).astype(BF16) for k in (0, 2, 3)]
    return cb + cw[1:2] * p + _dot(shifts, jnp.concatenate(taps, axis=0))


def _ssd_in_kernel(x_ref, mod_ref, g_ref, w_ref, wdt_ref, cw_ref, cb_ref, dtb_ref, sh_ref,
                   z_ref, xs_ref, b_ref, c_ref, dt_ref):
    mod = mod_ref[...]
    u = (_rms(x_ref[...], g_ref[...]) * (1.0 + mod[1:2]) + mod[0:1]).astype(BF16)
    z_ref[...] = _dot(u, w_ref[:, :SSD_INNER]).astype(z_ref.dtype)
    shifts = sh_ref[...]
    ch = 512
    for c0 in range(0, SSD_CONV_CH, ch):
        p = _dot(u, w_ref[:, SSD_INNER + c0:SSD_INNER + c0 + ch])
        y = _silu(_conv4(p, cw_ref[:, c0:c0 + ch], cb_ref[:, c0:c0 + ch], shifts))
        if c0 < SSD_INNER:
            dst, base, w = xs_ref, c0, SSD_GROUP_W
        elif c0 < SSD_INNER + SSD_BC:
            dst, base, w = b_ref, c0 - SSD_INNER, SSD_STATE
        else:
            dst, base, w = c_ref, c0 - SSD_INNER - SSD_BC, SSD_STATE
        for k in range(ch // w):
            dst[base // w + k] = y[:, k * w:(k + 1) * w].astype(dst.dtype)
    dt_ref[...] = _softplus(_dot(u, wdt_ref[...]) + dtb_ref[...])


def _ssd_in(tl, lay, x, mod, g, w_main, w_dt, conv_w, conv_b, dt_bias):
    T = tl.T
    gspec = lambda w: pl.BlockSpec((SSD_GROUPS, tl.tm, w), lambda i: (0, i, 0))
    return pl.pallas_call(
        _ssd_in_kernel,
        grid=(tl.n,),
        in_specs=[
            _row_spec(tl, D_MODEL), _mod_spec(tl), _full((1, D_MODEL)),
            _full(w_main.shape), _full(w_dt.shape), _full(conv_w.shape),
            _full(conv_b.shape), _full(dt_bias.shape), _shift_spec(tl),
        ],
        out_specs=[
            _row_spec(tl, SSD_INNER), gspec(SSD_GROUP_W), gspec(SSD_STATE), gspec(SSD_STATE),
            _row_spec(tl, 2 * LANE),
        ],
        out_shape=[
            jax.ShapeDtypeStruct((T, SSD_INNER), F32),
            jax.ShapeDtypeStruct((SSD_GROUPS, T, SSD_GROUP_W), BF16),
            jax.ShapeDtypeStruct((SSD_GROUPS, T, SSD_STATE), BF16),
            jax.ShapeDtypeStruct((SSD_GROUPS, T, SSD_STATE), BF16),
            jax.ShapeDtypeStruct((T, 2 * LANE), F32),
        ],
        compiler_params=_cparams(),
        name="ssd_in",
    )(x, mod, g, w_main, w_dt, conv_w, conv_b, dt_bias, _shift_mats(tl.tm, (lay.len_ctx, GRID_W)))


def _ssd_scan_kernel(bc_ref, seq_ref, first_ref, last_ref,
                     xf_ref, bf_ref, cf_ref, dtf_ref, xb_ref, bb_ref, cb_ref, dtb_ref,
                     alog_ref, h0_ref, yf_ref, yb_ref, hout_ref, h_s, cum_s, crow_s, wrow_s):
    i = pl.program_id(0)
    Q = SSD_CHUNK
    P = SSD_HEAD_DIM
    pairs_per_group = SSD_GROUP_HEADS // 2

    @pl.when(first_ref[i] == 1)
    def _():
        h_s[...] = h0_ref[...]

    rows = lax.broadcasted_iota(I32, (Q, Q), 0)
    cols = lax.broadcasted_iota(I32, (Q, Q), 1)
    masks = (rows >= cols, rows <= cols)
    first_head = lax.broadcasted_iota(I32, (1, LANE), 1) < P
    dirs = ((xf_ref, bf_ref, cf_ref, dtf_ref, yf_ref), (xb_ref, bb_ref, cb_ref, dtb_ref, yb_ref))

    for d in range(2):
        x_ref, b_ref, c_ref, dt_ref, y_ref = dirs[d]
        end = Q - 1 if d == 0 else 0
        dt = dt_ref[...]
        cum = _tri_cumsum(masks[d].astype(BF16), -dt * jnp.exp(alog_ref[d:d + 1]))
        cum_t = cum.T
        dt_t = dt.T
        cum_s[d] = cum
        crow_s[d] = cum_t - jnp.log(dt_t)
        wrow_s[d] = dt_t * jnp.exp(cum_t[:, end:end + 1] - cum_t)
        for g in range(SSD_GROUPS):
            bg = b_ref[g]
            cg = c_ref[g]
            cb = _dot_nt(cg, bg)
            cgf = cg.astype(F32)
            bgt = bg.astype(F32).T
            for j in range(pairs_per_group):
                p = g * pairs_per_group + j
                xp = x_ref[g, :, j * LANE:(j + 1) * LANE]
                zero = jnp.zeros_like(xp)
                xa = jnp.where(first_head, xp, zero)
                xb = jnp.where(first_head, zero, xp)
                hp = h_s[d, p]
                hpb = hp.astype(BF16)
                ha = jnp.where(first_head, hpb, jnp.zeros_like(hpb))
                hb = jnp.where(first_head, jnp.zeros_like(hpb), hpb)
                lhs, bws, tots = [], [], []
                for h in (2 * p, 2 * p + 1):
                    ccol = jnp.broadcast_to(cum_s[d, :, h:h + 1], (Q, Q))
                    ecol = jnp.exp(ccol)
                    dec = jnp.where(masks[d], jnp.exp(ccol - crow_s[d, h:h + 1, :]), 0.0)
                    lhs += [(cb * dec).astype(BF16), (cgf * ecol).astype(BF16)]
                    bws.append((bgt * wrow_s[d, h:h + 1, :]).astype(BF16))
                    tots.append(ecol[end:end + 1])
                y = _dot(jnp.concatenate(lhs, axis=1), jnp.concatenate([xa, ha, xb, hb], axis=0))
                y_ref[g, :, j * LANE:(j + 1) * LANE] = y.astype(y_ref.dtype)
                upd = _dot(jnp.concatenate(bws, axis=1), jnp.concatenate([xa, xb], axis=0))
                h_s[d, p] = hp * jnp.where(first_head, tots[0], tots[1]) + upd

    @pl.when(last_ref[i] == 1)
    def _():
        hout_ref[...] = h_s[...]


def _scan_tables(lay, blk):
    bc, seq, first, last = [], [], [], []
    s0 = 0
    sid = 0
    for n, L in ((lay.n_ctx, lay.len_ctx), (lay.n_lat, lay.len_lat)):
        nb = L // blk
        for _ in range(n):
            for k in range(nb):
                bc.append(s0 + nb - 1 - k)
                seq.append(sid)
                first.append(int(k == 0))
                last.append(int(k == nb - 1))
            s0 += nb
            sid += 1
    return tuple(jnp.asarray(np.asarray(a, np.int32)) for a in (bc, seq, first, last))


def _ssd_scan(lay, xs, bm, cm, dt, a_log, h0):
    Q = SSD_CHUNK
    T = xs.shape[1]
    tabs = _scan_tables(lay, Q)
    nseq = lay.n_ctx + lay.n_lat
    G, N, W = SSD_GROUPS, SSD_STATE, SSD_GROUP_W
    st_shape = (2, SSD_HEADS // 2, N, 2 * SSD_HEAD_DIM)
    assert 2 * SSD_HEAD_DIM == LANE and Q == LANE
    fwd = lambda w: pl.BlockSpec((G, Q, w), lambda i, bc, sq, fi, la: (0, i, 0))
    bwd = lambda w: pl.BlockSpec((G, Q, w), lambda i, bc, sq, fi, la: (0, bc[i], 0))
    st_spec = pl.BlockSpec((None,) + st_shape, lambda i, bc, sq, fi, la: (sq[i], 0, 0, 0, 0))
    grid_spec = pltpu.PrefetchScalarGridSpec(
        num_scalar_prefetch=4,
        grid=(T // Q,),
        in_specs=[
            fwd(W), fwd(N), fwd(N), pl.BlockSpec((Q, LANE), lambda i, bc, sq, fi, la: (i, 0)),
            bwd(W), bwd(N), bwd(N), pl.BlockSpec((Q, LANE), lambda i, bc, sq, fi, la: (bc[i], 1)),
            pl.BlockSpec((2, LANE), lambda i, *_: (0, 0)),
            st_spec,
        ],
        out_specs=[fwd(W), bwd(W), st_spec],
        scratch_shapes=[pltpu.VMEM(st_shape, F32), pltpu.VMEM((2, Q, LANE), F32),
                        pltpu.VMEM((2, LANE, Q), F32), pltpu.VMEM((2, LANE, Q), F32)],
    )
    return pl.pallas_call(
        _ssd_scan_kernel,
        grid_spec=grid_spec,
        out_shape=[
            jax.ShapeDtypeStruct((G, T, W), F32),
            jax.ShapeDtypeStruct((G, T, W), F32),
            jax.ShapeDtypeStruct((nseq,) + st_shape, F32),
        ],
        compiler_params=_cparams(),
        name="ssd_scan",
    )(*tabs, xs, bm, cm, dt, xs, bm, cm, dt, a_log, h0)


def _ssd_out_kernel(yf_ref, yb_ref, xs_ref, z_ref, x_ref, mod_ref, d_ref, ng_ref, w_ref, o_ref):
    parts = []
    for g in range(SSD_GROUPS):
        dg = d_ref[:, g * SSD_GROUP_W:(g + 1) * SSD_GROUP_W]
        parts.append(yf_ref[g].astype(F32) + yb_ref[g].astype(F32) + dg * xs_ref[g].astype(F32))
    y = jnp.concatenate(parts, axis=1)
    y = _rms(y * _silu(z_ref[...].astype(F32)), ng_ref[...]).astype(BF16)
    o_ref[...] = x_ref[...] + mod_ref[2:3] * _dot(y, w_ref[...])


def _ssd_out(tl, yf, yb, xs, z, x, mod, d_exp, norm_g, w_out):
    gspec = pl.BlockSpec((SSD_GROUPS, tl.tm, SSD_GROUP_W), lambda i: (0, i, 0))
    return pl.pallas_call(
        _ssd_out_kernel,
        grid=(tl.n,),
        in_specs=[gspec, gspec, gspec, _row_spec(tl, SSD_INNER), _row_spec(tl, D_MODEL),
                  _mod_spec(tl), _full(d_exp.shape), _full(norm_g.shape), _full(w_out.shape)],
        out_specs=_row_spec(tl, D_MODEL),
        out_shape=jax.ShapeDtypeStruct((tl.T, D_MODEL), F32),
        compiler_params=_cparams(),
        name="ssd_out",
    )(yf, yb, xs, z, x, mod, d_exp, norm_g, w_out)


def _ssd_layer(tl, lay, x, mod, g, h0, w_in, conv_w, conv_b, a_log, dt_bias, d_skip, norm_g, w_out):
    nseq = h0.shape[0]
    split = SSD_INNER + SSD_CONV_CH
    w_main = w_in[:, :split].astype(BF16)
    lane_pad = lambda v: jnp.pad(v, ((0, 0), (0, 0), (0, LANE - SSD_HEADS)))
    w_dt = lane_pad(w_in[:, split:].reshape(D_MODEL, 2, SSD_HEADS)).reshape(D_MODEL, 2 * LANE).astype(BF16)
    dtb = lane_pad(dt_bias.reshape(1, 2, SSD_HEADS)).reshape(1, 2 * LANE)
    alog = lane_pad(a_log.reshape(1, 2, SSD_HEADS)).reshape(2, LANE)
    z, xs, bm, cm, dt = _ssd_in(tl, lay, x, mod, g.reshape(1, -1), w_main, w_dt,
                                conv_w, conv_b.reshape(1, -1), dtb)
    pair_shape = (nseq, 2, SSD_HEADS // 2, 2 * SSD_HEAD_DIM, SSD_STATE)
    yf, yb, hfin = _ssd_scan(lay, xs, bm, cm, dt, alog, h0.reshape(pair_shape).swapaxes(-1, -2))
    hfin = hfin.swapaxes(-1, -2).reshape(h0.shape)
    d_exp = jnp.repeat(d_skip, SSD_HEAD_DIM).reshape(1, -1)
    x = _ssd_out(tl, yf, yb, xs, z, x, mod, d_exp, norm_g.reshape(1, -1), w_out.astype(BF16))
    return x, hfin


def _gla_in_kernel(x_ref, mod_ref, g_ref, w_ref, wgd_ref, wup_ref, bg_ref,
                   q_ref, k_ref, v_ref, r_ref, la_ref):
    mod = mod_ref[...]
    u = (_rms(x_ref[...], g_ref[...]) * (1.0 + mod[1:2]) + mod[0:1]).astype(BF16)
    q_ref[...] = (_dot(u, w_ref[:, :GLA_KEY]) * (GLA_DK ** -0.5)).astype(q_ref.dtype)
    k_ref[...] = _dot(u, w_ref[:, GLA_KEY:2 * GLA_KEY]).astype(k_ref.dtype)
    v_ref[...] = _dot(u, w_ref[:, 2 * GLA_KEY:2 * GLA_KEY + GLA_VAL]).astype(v_ref.dtype)
    r_ref[...] = _dot(u, w_ref[:, 2 * GLA_KEY + GLA_VAL:]).astype(r_ref.dtype)
    gd = _dot(u, wgd_ref[...]).astype(BF16)
    logit = _dot(gd, wup_ref[...]) + bg_ref[...]
    la_ref[...] = (jnp.minimum(logit, 0.0) - jnp.log(1.0 + jnp.exp(-jnp.abs(logit)))) / GLA_TAU


def _gla_in(tl, x, mod, g, w_main, w_gd, w_up, b_gate):
    T = tl.T
    return pl.pallas_call(
        _gla_in_kernel,
        grid=(tl.n,),
        in_specs=[_row_spec(tl, D_MODEL), _mod_spec(tl), _full((1, D_MODEL)), _full(w_main.shape),
                  _full(w_gd.shape), _full(w_up.shape), _full(b_gate.shape)],
        out_specs=[_row_spec(tl, GLA_KEY), _row_spec(tl, GLA_KEY), _row_spec(tl, GLA_VAL),
                   _row_spec(tl, GLA_VAL), _row_spec(tl, 2 * GLA_KEY)],
        out_shape=[
            jax.ShapeDtypeStruct((T, GLA_KEY), BF16),
            jax.ShapeDtypeStruct((T, GLA_KEY), BF16),
            jax.ShapeDtypeStruct((T, GLA_VAL), BF16),
            jax.ShapeDtypeStruct((T, GLA_VAL), F32),
            jax.ShapeDtypeStruct((T, 2 * GLA_KEY), F32),
        ],
        compiler_params=_cparams(),
        name="gla_in",
    )(x, mod, g, w_main, w_gd, w_up, b_gate)


def _gla_scan_kernel(bc_ref, seq_ref, first_ref, last_ref,
                     qf_ref, kf_ref, vf_ref, laf_ref, qb_ref, kb_ref, vb_ref, lab_ref, s0_ref,
                     of_ref, ob_ref, sout_ref, s_s):
    i = pl.program_id(0)
    C = GLA_CHUNK
    nsub = qf_ref.shape[0] // C

    @pl.when(first_ref[i] == 1)
    def _():
        s_s[...] = s0_ref[...]

    nrow = qf_ref.shape[0]
    rows = lax.broadcasted_iota(I32, (nrow, nrow), 0)
    cols = lax.broadcasted_iota(I32, (nrow, nrow), 1)
    shift = C.bit_length() - 1
    assert C == 1 << shift
    same_chunk = jnp.right_shift(rows, shift) == jnp.right_shift(cols, shift)
    tris = (jnp.where(same_chunk & (rows >= cols), 1.0, 0.0).astype(BF16),
            jnp.where(same_chunk & (rows <= cols), 1.0, 0.0).astype(BF16))
    rows_c = lax.broadcasted_iota(I32, (C, C), 0)
    cols_c = lax.broadcasted_iota(I32, (C, C), 1)
    masks = (rows_c >= cols_c, rows_c <= cols_c)
    dirs = ((qf_ref, kf_ref, vf_ref, laf_ref, of_ref), (qb_ref, kb_ref, vb_ref, lab_ref, ob_ref))

    for d in range(2):
        q_ref, k_ref, v_ref, la_ref, o_ref = dirs[d]
        gcum_all = _tri_cumsum(tris[d], la_ref[:, d * GLA_KEY:(d + 1) * GLA_KEY])
        for c in range(nsub):
            cc = c if d == 0 else nsub - 1 - c
            rs = slice(cc * C, (cc + 1) * C)
            for h in range(GLA_HEADS):
                ks = slice(h * GLA_DK, (h + 1) * GLA_DK)
                vs = slice(h * GLA_DV, (h + 1) * GLA_DV)
                gcum = gcum_all[rs, ks]
                qh = q_ref[rs, ks].astype(F32)
                kh = k_ref[rs, ks].astype(F32)
                vh = v_ref[rs, vs]
                qi = (qh * jnp.exp(gcum)).astype(BF16)
                ki = (kh * jnp.exp(-gcum)).astype(BF16)
                att = jnp.where(masks[d], _dot_nt(qi, ki), 0.0).astype(BF16)
                st = s_s[d, h]
                o_ref[rs, vs] = (_dot(att, vh) + _dot_nt(qi, st.astype(BF16))).astype(o_ref.dtype)
                glast = gcum[C - 1:C] if d == 0 else gcum[0:1]
                kdec = (kh * jnp.exp(glast - gcum)).astype(BF16)
                s_s[d, h] = st * jnp.exp(glast) + _dot_tn(vh, kdec)

    @pl.when(last_ref[i] == 1)
    def _():
        sout_ref[...] = s_s[...]


def _gla_scan(lay, q, k, v, la, s0):
    B = SCAN_BLK
    T = q.shape[0]
    tabs = _scan_tables(lay, B)
    nseq = lay.n_ctx + lay.n_lat
    fwd = lambda w: pl.BlockSpec((B, w), lambda i, bc, sq, fi, la_: (i, 0))
    bwd = lambda w: pl.BlockSpec((B, w), lambda i, bc, sq, fi, la_: (bc[i], 0))
    st_spec = pl.BlockSpec((None, 2, GLA_HEADS, GLA_DV, GLA_DK),
                           lambda i, bc, sq, fi, la_: (sq[i], 0, 0, 0, 0))
    grid_spec = pltpu.PrefetchScalarGridSpec(
        num_scalar_prefetch=4,
        grid=(T // B,),
        in_specs=[fwd(GLA_KEY), fwd(GLA_KEY), fwd(GLA_VAL), fwd(2 * GLA_KEY),
                  bwd(GLA_KEY), bwd(GLA_KEY), bwd(GLA_VAL), bwd(2 * GLA_KEY), st_spec],
        out_specs=[fwd(GLA_VAL), bwd(GLA_VAL), st_spec],
        scratch_shapes=[pltpu.VMEM((2, GLA_HEADS, GLA_DV, GLA_DK), F32)],
    )
    return pl.pallas_call(
        _gla_scan_kernel,
        grid_spec=grid_spec,
        out_shape=[
            jax.ShapeDtypeStruct((T, GLA_VAL), F32),
            jax.ShapeDtypeStruct((T, GLA_VAL), F32),
            jax.ShapeDtypeStruct((nseq, 2, GLA_HEADS, GLA_DV, GLA_DK), F32),
        ],
        compiler_params=_cparams(),
        name="gla_scan",
    )(*tabs, q, k, v, la, q, k, v, la, s0)


def _gla_out_kernel(of_ref, ob_ref, r_ref, x_ref, mod_ref, ng_ref, w_ref, o_ref):
    o = of_ref[...].astype(F32) + ob_ref[...].astype(F32)
    ng = ng_ref[...]
    parts = [_rms(o[:, h * GLA_DV:(h + 1) * GLA_DV], ng) for h in range(GLA_HEADS)]
    y = (jnp.concatenate(parts, axis=1) * _silu(r_ref[...].astype(F32))).astype(BF16)
    o_ref[...] = x_ref[...] + mod_ref[2:3] * _dot(y, w_ref[...])


def _gla_out(tl, of, ob, r, x, mod, norm_g, w_out):
    return pl.pallas_call(
        _gla_out_kernel,
        grid=(tl.n,),
        in_specs=[_row_spec(tl, GLA_VAL), _row_spec(tl, GLA_VAL), _row_spec(tl, GLA_VAL),
                  _row_spec(tl, D_MODEL), _mod_spec(tl), _full(norm_g.shape), _full(w_out.shape)],
        out_specs=_row_spec(tl, D_MODEL),
        out_shape=jax.ShapeDtypeStruct((tl.T, D_MODEL), F32),
        compiler_params=_cparams(),
        name="gla_out",
    )(of, ob, r, x, mod, norm_g, w_out)


def _gla_layer(tl, lay, x, mod, g, s0, w_in, w_gate_up, b_gate, norm_g, w_out):
    split = 2 * GLA_KEY + 2 * GLA_VAL
    w_main = w_in[:, :split].astype(BF16)
    w_gd = jnp.pad(w_in[:, split:], ((0, 0), (0, LANE - 2 * GLA_RANK))).astype(BF16)
    w_up = jnp.zeros((LANE, 2 * GLA_KEY), F32)
    for d in range(2):
        w_up = w_up.at[d * GLA_RANK:(d + 1) * GLA_RANK, d * GLA_KEY:(d + 1) * GLA_KEY].set(w_gate_up[d])
    q, k, v, r, la = _gla_in(tl, x, mod, g.reshape(1, -1), w_main, w_gd, w_up.astype(BF16),
                             b_gate.reshape(1, -1))
    of, ob, sfin = _gla_scan(lay, q, k, v, la, s0.swapaxes(-1, -2))
    x = _gla_out(tl, of, ob, r, x, mod, norm_g.reshape(1, -1), w_out.astype(BF16))
    return x, sfin.swapaxes(-1, -2)


def _gelu_tanh(x):
    return 0.5 * x * (1.0 + jnp.tanh(0.7978845608028654 * (x + 0.044715 * (x * x * x))))


def _lru_in_kernel(x_ref, mod_ref, g_ref, w_ref, cw_ref, cb_ref, wg_ref, bg_ref, lam_ref, sh_ref,
                   a_ref, b_ref, gg_ref):
    W = LRU_WIDTH
    mod = mod_ref[...]
    u = (_rms(x_ref[...], g_ref[...]) * (1.0 + mod[1:2]) + mod[0:1]).astype(BF16)
    gg_ref[...] = _gelu_tanh(_dot(u, w_ref[:, :W])).astype(gg_ref.dtype)
    xb = _conv4(_dot(u, w_ref[:, W:]), cw_ref[...], cb_ref[...], sh_ref[...])
    sp = _softplus(-lam_ref[...])
    for n in range(LRU_BLOCKS):
        cs = slice(n * LRU_BLOCK, (n + 1) * LRU_BLOCK)
        xn = xb[:, cs]
        gates = jax.nn.sigmoid(_dot(xn.astype(BF16), wg_ref[n]) + bg_ref[n])
        for d in range(2):
            r = gates[:, d * LRU_BLOCK:(d + 1) * LRU_BLOCK]
            ig = gates[:, (2 + d) * LRU_BLOCK:(3 + d) * LRU_BLOCK]
            log_a = (-LRU_C) * r * sp[:, d * W + n * LRU_BLOCK:d * W + (n + 1) * LRU_BLOCK]
            a = jnp.exp(log_a)
            ds = slice(d * W + n * LRU_BLOCK, d * W + (n + 1) * LRU_BLOCK)
            a_ref[:, ds] = a
            b_ref[:, ds] = jnp.sqrt(1.0 - a * a) * (ig * xn)


def _lru_in(tl, lay, x, mod, g, w_in, conv_w, conv_b, w_gates, b_gates, lam):
    T = tl.T
    return pl.pallas_call(
        _lru_in_kernel,
        grid=(tl.n,),
        in_specs=[_row_spec(tl, D_MODEL), _mod_spec(tl), _full((1, D_MODEL)), _full(w_in.shape),
                  _full(conv_w.shape), _full(conv_b.shape), _full(w_gates.shape),
                  _full(b_gates.shape), _full(lam.shape), _shift_spec(tl)],
        out_specs=[_row_spec(tl, 2 * LRU_WIDTH), _row_spec(tl, 2 * LRU_WIDTH), _row_spec(tl, LRU_WIDTH)],
        out_shape=[
            jax.ShapeDtypeStruct((T, 2 * LRU_WIDTH), F32),
            jax.ShapeDtypeStruct((T, 2 * LRU_WIDTH), F32),
            jax.ShapeDtypeStruct((T, LRU_WIDTH), F32),
        ],
        compiler_params=_cparams(),
        name="lru_in",
    )(x, mod, g, w_in, conv_w, conv_b, w_gates, b_gates, lam, _shift_mats(tl.tm, (lay.len_ctx, GRID_W)))


def _lru_scan_kernel(bc_ref, seq_ref, first_ref, last_ref,
                     af_ref, bf_ref, ab_ref, bb_ref, h0_ref, hf_ref, hb_ref, hout_ref, h_s):
    i = pl.program_id(0)
    S = SUBLANE
    W = LRU_WIDTH
    ntile = af_ref.shape[0] // S

    @pl.when(first_ref[i] == 1)
    def _():
        h_s[...] = h0_ref[...]

    ri = lax.broadcasted_iota(I32, (S, W), 0)

    def tile(j, carry):
        hf, hb = carry
        rs = pl.ds(pl.multiple_of(j * S, S), S)
        a = af_ref[rs, 0:W]
        b = bf_ref[rs, 0:W]
        for sh in (1, 2, 4):
            keep = ri >= sh
            a_s = jnp.where(keep, pltpu.roll(a, sh, 0), 1.0)
            b_s = jnp.where(keep, pltpu.roll(b, sh, 0), 0.0)
            b = a * b_s + b
            a = a * a_s
        h8 = a * hf + b
        hf_ref[rs, :] = h8
        hf = h8[S - 1:S]
        rs = pl.ds(pl.multiple_of((ntile - 1 - j) * S, S), S)
        a = ab_ref[rs, W:2 * W]
        b = bb_ref[rs, W:2 * W]
        for sh in (1, 2, 4):
            keep = ri < S - sh
            a_s = jnp.where(keep, pltpu.roll(a, S - sh, 0), 1.0)
            b_s = jnp.where(keep, pltpu.roll(b, S - sh, 0), 0.0)
            b = a * b_s + b
            a = a * a_s
        h8 = a * hb + b
        hb_ref[rs, :] = h8
        hb = h8[0:1]
        return hf, hb

    hf, hb = lax.fori_loop(0, ntile, tile, (h_s[0:1], h_s[1:2]))
    h_s[0:1] = hf
    h_s[1:2] = hb

    @pl.when(last_ref[i] == 1)
    def _():
        hout_ref[...] = h_s[...]


def _lru_scan(lay, a, b, h0):
    B = SCAN_BLK
    T = a.shape[0]
    W = LRU_WIDTH
    tabs = _scan_tables(lay, B)
    nseq = lay.n_ctx + lay.n_lat
    fwd = lambda w: pl.BlockSpec((B, w), lambda i, bc, sq, fi, la_: (i, 0))
    bwd = lambda w: pl.BlockSpec((B, w), lambda i, bc, sq, fi, la_: (bc[i], 0))
    st_spec = pl.BlockSpec((None, 2, W), lambda i, bc, sq, fi, la_: (sq[i], 0, 0))
    grid_spec = pltpu.PrefetchScalarGridSpec(
        num_scalar_prefetch=4,
        grid=(T // B,),
        in_specs=[fwd(2 * W), fwd(2 * W), bwd(2 * W), bwd(2 * W), st_spec],
        out_specs=[fwd(W), bwd(W), st_spec],
        scratch_shapes=[pltpu.VMEM((2, W), F32)],
    )
    return pl.pallas_call(
        _lru_scan_kernel,
        grid_spec=grid_spec,
        out_shape=[
            jax.ShapeDtypeStruct((T, W), F32),
            jax.ShapeDtypeStruct((T, W), F32),
            jax.ShapeDtypeStruct((nseq, 2, W), F32),
        ],
        compiler_params=_cparams(),
        name="lru_scan",
    )(*tabs, a, b, a, b, h0)


def _lru_out_kernel(hf_ref, hb_ref, gg_ref, x_ref, mod_ref, w_ref, o_ref):
    y = ((hf_ref[...] + hb_ref[...]) * gg_ref[...].astype(F32)).astype(BF16)
    o_ref[...] = x_ref[...] + mod_ref[2:3] * _dot(y, w_ref[...])


def _lru_out(tl, hf, hb, gg, x, mod, w_out):
    W = LRU_WIDTH
    return pl.pallas_call(
        _lru_out_kernel,
        grid=(tl.n,),
        in_specs=[_row_spec(tl, W), _row_spec(tl, W), _row_spec(tl, W), _row_spec(tl, D_MODEL),
                  _mod_spec(tl), _full(w_out.shape)],
        out_specs=_row_spec(tl, D_MODEL),
        out_shape=jax.ShapeDtypeStruct((tl.T, D_MODEL), F32),
        compiler_params=_cparams(),
        name="lru_out",
    )(hf, hb, gg, x, mod, w_out)


def _lru_layer(tl, lay, x, mod, g, h0, w_in, conv_w, conv_b, w_a, b_a, w_x, b_x, lam, w_out):
    w_gates = jnp.concatenate([w_a[0], w_a[1], w_x[0], w_x[1]], axis=-1).astype(BF16)
    blk = lambda v: v.reshape(2, LRU_BLOCKS, LRU_BLOCK)
    b_gates = jnp.concatenate([blk(b_a)[0], blk(b_a)[1], blk(b_x)[0], blk(b_x)[1]], axis=-1)
    b_gates = b_gates.reshape(LRU_BLOCKS, 1, 4 * LRU_BLOCK)
    a, b, gg = _lru_in(tl, lay, x, mod, g.reshape(1, -1), w_in.astype(BF16), conv_w,
                       conv_b.reshape(1, -1), w_gates, b_gates, lam.reshape(1, -1))
    hf, hb, hfin = _lru_scan(lay, a, b, h0)
    x = _lru_out(tl, hf, hb, gg, x, mod, w_out.astype(BF16))
    return x, hfin


def _swiglu(u, w1_ref, w3_ref, w2_ref):
    a = _dot(u, w1_ref[...])
    b = _dot(u, w3_ref[...])
    return _dot((_silu(a) * b).astype(BF16), w2_ref[...])


def _ffn_kernel(x_ref, mod_ref, g_ref, w1_ref, w3_ref, w2_ref, o_ref):
    x = x_ref[...]
    mod = mod_ref[...]
    u = (_rms(x, g_ref[...]) * (1.0 + mod[4:5]) + mod[3:4]).astype(BF16)
    o_ref[...] = x + mod[5:6] * _swiglu(u, w1_ref, w3_ref, w2_ref)


def _ffn_layer(tl, x, mod, g, w1, w3, w2):
    w1, w3, w2 = w1.astype(BF16), w3.astype(BF16), w2.astype(BF16)
    return pl.pallas_call(
        _ffn_kernel,
        grid=(tl.n,),
        in_specs=[_row_spec(tl, D_MODEL), _mod_spec(tl), _full((1, D_MODEL)),
                  _full(w1.shape), _full(w3.shape), _full(w2.shape)],
        out_specs=_row_spec(tl, D_MODEL),
        out_shape=jax.ShapeDtypeStruct((tl.T, D_MODEL), F32),
        compiler_params=_cparams(),
        name="ffn",
    )(x, mod, g.reshape(1, -1), w1, w3, w2)


SLAB = D_MODEL // LANE
assert SLAB == SUBLANE


def _to_slab(ref, val):
    tm = val.shape[0]
    for s in range(SLAB):
        ref[pl.ds(s, tm, stride=SLAB), :] = val[:, s * LANE:(s + 1) * LANE]


def _from_slab(ref, tm):
    return jnp.concatenate([ref[pl.ds(s, tm, stride=SLAB), :] for s in range(SLAB)], axis=1)


def _moe_route_kernel(x_ref, mod_ref, g_ref, wr_ref, u_ref, mi_ref, mf_ref, cnt_ref, cnt_s):
    i = pl.program_id(0)
    tm = x_ref.shape[0]

    @pl.when(i == 0)
    def _():
        cnt_s[...] = jnp.zeros_like(cnt_s)

    mod = mod_ref[...]
    u = _rms(x_ref[...], g_ref[...]) * (1.0 + mod[4:5]) + mod[3:4]
    _to_slab(u_ref, u)
    logits = jnp.dot(u, wr_ref[...], preferred_element_type=F32, precision=lax.Precision.HIGHEST)
    lane = lax.broadcasted_iota(I32, (tm, LANE), 1)
    lg = jnp.where(lane < N_EXPERTS, logits, -jnp.inf)
    m1 = jnp.max(lg, axis=1, keepdims=True)
    i1 = jnp.min(jnp.where(lg == m1, lane, LANE), axis=1, keepdims=True)
    lg2 = jnp.where(lane == i1, -jnp.inf, lg)
    m2 = jnp.max(lg2, axis=1, keepdims=True)
    i2 = jnp.min(jnp.where(lg2 == m2, lane, LANE), axis=1, keepdims=True)
    e2 = jnp.exp(m2 - m1)
    g1 = 1.0 / (1.0 + e2)
    g2 = e2 / (1.0 + e2)
    oh1 = lane == i1
    oh2 = lane == i2
    cnt = (oh1 | oh2).astype(BF16)
    rr = lax.broadcasted_iota(I32, (tm, tm), 0)
    cc = lax.broadcasted_iota(I32, (tm, tm), 1)
    before = _dot((rr > cc).astype(BF16), cnt) + cnt_s[...]
    r1 = jnp.sum(jnp.where(oh1, before, 0.0), axis=1, keepdims=True).astype(I32)
    r2 = jnp.sum(jnp.where(oh2, before, 0.0), axis=1, keepdims=True).astype(I32)
    cnt_s[...] = cnt_s[...] + jnp.sum(cnt.astype(F32), axis=0, keepdims=True)
    mi_ref[...] = jnp.where(lane == 0, i1, jnp.where(lane == 1, i2, jnp.where(lane == 2, r1, r2)))
    mf_ref[...] = jnp.where(lane == 0, g1, g2)
    cnt_ref[...] = cnt_s[...]


def _moe_route(tl, x, mod, g, w_router):
    T = tl.T
    return pl.pallas_call(
        _moe_route_kernel,
        grid=(tl.n,),
        in_specs=[_row_spec(tl, D_MODEL), _mod_spec(tl), _full((1, D_MODEL)), _full(w_router.shape)],
        out_specs=[pl.BlockSpec((tl.tm * SLAB, LANE), lambda i: (i, 0)),
                   _row_spec(tl, LANE), _row_spec(tl, LANE), _full((1, LANE))],
        out_shape=[
            jax.ShapeDtypeStruct((T * SLAB, LANE), F32),
            jax.ShapeDtypeStruct((T, LANE), I32),
            jax.ShapeDtypeStruct((T, LANE), F32),
            jax.ShapeDtypeStruct((1, LANE), F32),
        ],
        scratch_shapes=[pltpu.VMEM((1, LANE), F32)],
        compiler_params=_cparams(),
        name="moe_route",
    )(x, mod, g, w_router)


DISPATCH_BLK = 512


def _moe_dispatch_kernel(pos_ref, tail_ref, u_ref, z_hbm, xs_hbm, sem, *, nblk, tm, nt):
    i = pl.program_id(0)

    def rows(t, n=1):
        return pl.ds(pl.multiple_of(t * SLAB, SLAB), n * SLAB)

    def issue(r, carry):
        for k in range(2):
            pltpu.make_async_copy(u_ref.at[rows(r)], xs_hbm.at[rows(pos_ref[2 * r + k])],
                                  sem).start(priority=k)
        return carry

    lax.fori_loop(0, nblk, issue, 0, unroll=8)
    for k in range(2):
        pltpu.make_async_copy(u_ref, xs_hbm.at[rows(0, nblk)], sem).wait()

    @pl.when(i == pl.num_programs(0) - 1)
    def _():
        for e in range(N_EXPERTS):
            start = tail_ref[e]
            n = tail_ref[N_EXPERTS + e]

            def fill(r, carry):
                pltpu.make_async_copy(z_hbm.at[rows(0)], xs_hbm.at[rows(start + r)], sem).start()
                return carry

            def drain(r, carry):
                pltpu.make_async_copy(z_hbm.at[rows(0)], xs_hbm.at[rows(start + r)], sem).wait()
                return carry

            lax.fori_loop(0, n, fill, 0)
            lax.fori_loop(0, n, drain, 0)

        def fill_tile(j, carry):
            dst = xs_hbm.at[rows(j * tm, tm)]
            pltpu.make_async_copy(z_hbm, dst, sem).start()
            pltpu.make_async_copy(z_hbm, dst, sem).wait()
            return carry

        lax.fori_loop(tail_ref[2 * N_EXPERTS], nt, fill_tile, 0)


def _moe_dispatch(T, P, tm, pos, tail, u_slab):
    nblk = DISPATCH_BLK
    assert T % nblk == 0 and P % tm == 0
    kern = functools.partial(_moe_dispatch_kernel, nblk=nblk, tm=tm, nt=P // tm)
    zeros = jnp.zeros((tm * SLAB, LANE), F32)
    return pl.pallas_call(
        kern,
        grid=(T // nblk,),
        in_specs=[
            pl.BlockSpec((2 * nblk,), lambda i: (i,), memory_space=pltpu.SMEM),
            pl.BlockSpec(memory_space=pltpu.SMEM),
            pl.BlockSpec((nblk * SLAB, LANE), lambda i: (i, 0)),
            pl.BlockSpec(memory_space=pl.ANY),
        ],
        out_specs=pl.BlockSpec(memory_space=pl.ANY),
        out_shape=jax.ShapeDtypeStruct((P * SLAB, LANE), F32),
        scratch_shapes=[pltpu.SemaphoreType.DMA(())],
        compiler_params=_cparams(),
        name="moe_dispatch",
    )(pos, tail, u_slab, zeros)


def _moe_ffn_kernel(texp_ref, nv_ref, xs_ref, w1_ref, w3_ref, w2_ref, ys_ref):
    j = pl.program_id(0)
    tm = xs_ref.shape[0] // SLAB
    valid = j < nv_ref[0]

    @pl.when(valid)
    def _():
        u = _from_slab(xs_ref, tm).astype(BF16)
        _to_slab(ys_ref, _swiglu(u, w1_ref, w3_ref, w2_ref))

    @pl.when(jnp.logical_not(valid))
    def _():
        ys_ref[...] = jnp.zeros_like(ys_ref)


def _moe_ffn(tm, nt, texp, nv, xs2d, w1, w3, w2):
    wspec = lambda shp: pl.BlockSpec((None,) + shp, lambda j, te, nv_: (te[j], 0, 0))
    grid_spec = pltpu.PrefetchScalarGridSpec(
        num_scalar_prefetch=2,
        grid=(nt,),
        in_specs=[
            pl.BlockSpec((tm * SLAB, LANE), lambda j, te, nv_: (jnp.minimum(j, nv_[0] - 1), 0)),
            wspec(w1.shape[1:]), wspec(w3.shape[1:]), wspec(w2.shape[1:]),
        ],
        out_specs=pl.BlockSpec((tm * SLAB, LANE), lambda j, te, nv_: (j, 0)),
    )
    return pl.pallas_call(
        _moe_ffn_kernel,
        grid_spec=grid_spec,
        out_shape=jax.ShapeDtypeStruct(xs2d.shape, F32),
        compiler_params=_cparams(),
        name="moe_ffn",
    )(texp, nv, xs2d, w1, w3, w2)


def _moe_combine_kernel(pos_ref, x_ref, mod_ref, mf_ref, ys_hbm, o_ref, buf0, buf1, sem):
    tm = x_ref.shape[0]
    bufs = (buf0, buf1)

    def issue(r, carry):
        for k in range(2):
            dst = bufs[k].at[pl.ds(pl.multiple_of(r * SLAB, SLAB), SLAB)]
            pltpu.make_async_copy(ys_hbm.at[pos_ref[2 * r + k]], dst, sem).start()
        return carry

    lax.fori_loop(0, tm, issue, 0)
    for k in range(2):
        pltpu.make_async_copy(bufs[1 - k], bufs[k], sem).wait()
    mf = mf_ref[...]
    g1 = mf[:, 0:1]
    g2 = mf[:, 1:2]
    for s in range(SLAB):
        cs = slice(s * LANE, (s + 1) * LANE)
        y = g1 * buf0[pl.ds(s, tm, stride=SLAB), :] + g2 * buf1[pl.ds(s, tm, stride=SLAB), :]
        o_ref[:, cs] = x_ref[:, cs] + mod_ref[5:6, cs] * y


def _moe_combine(tl, pos, x, mod, mf, ys):
    tm = tl.tm
    return pl.pallas_call(
        _moe_combine_kernel,
        grid=(tl.n,),
        in_specs=[
            pl.BlockSpec((2 * tm,), lambda i: (i,), memory_space=pltpu.SMEM),
            _row_spec(tl, D_MODEL), _mod_spec(tl), _row_spec(tl, LANE),
            pl.BlockSpec(memory_space=pl.ANY),
        ],
        out_specs=_row_spec(tl, D_MODEL),
        out_shape=jax.ShapeDtypeStruct((tl.T, D_MODEL), F32),
        scratch_shapes=[pltpu.VMEM((tm * SLAB, LANE), F32), pltpu.VMEM((tm * SLAB, LANE), F32),
                        pltpu.SemaphoreType.DMA(())],
        compiler_params=_cparams(),
        name="moe_combine",
    )(pos, x, mod, mf, ys)


def _moe_layer(tl, x, mod, g, w_router, w1, w3, w2):
    T, tm = tl.T, tl.tm
    nt = 2 * tl.n + N_EXPERTS
    P = nt * tm
    wr = jnp.pad(w_router, ((0, 0), (0, LANE - N_EXPERTS)))
    u_slab, mi, mf, cnt = _moe_route(tl, x, mod, g.reshape(1, -1), wr)
    counts = cnt[0, :N_EXPERTS].astype(I32)
    tiles_e = (counts + tm - 1) // tm
    cum_tiles = jnp.cumsum(tiles_e)
    nv = cum_tiles[-1:]
    off = (cum_tiles - tiles_e) * tm
    sel = mi[:, 0:2, None] == jnp.arange(N_EXPERTS, dtype=I32)
    pos = (jnp.sum(jnp.where(sel, off, 0), axis=-1) + mi[:, 2:4]).reshape(2 * T)
    jj = jnp.minimum(jnp.arange(nt, dtype=I32), nv[0] - 1)
    texp = jnp.sum((jj[:, None] >= cum_tiles[None, :]).astype(I32), axis=1)
    tail = jnp.concatenate([off + counts, tiles_e * tm - counts, nv])
    xs = _moe_dispatch(T, P, tm, pos, tail, u_slab)
    ys = _moe_ffn(tm, nt, texp, nv, xs,
                  w1.astype(BF16), w3.astype(BF16), w2.astype(BF16))
    return _moe_combine(tl, pos, x, mod, mf, ys.reshape(P, SLAB, LANE))


def _final_kernel(x_ref, g_ref, o_ref):
    o_ref[...] = _rms(x_ref[...], g_ref[...])


def _final_norm(tl, x, g):
    return pl.pallas_call(
        _final_kernel,
        grid=(tl.n,),
        in_specs=[_row_spec(tl, D_MODEL), _full((1, D_MODEL))],
        out_specs=_row_spec(tl, D_MODEL),
        out_shape=jax.ShapeDtypeStruct((tl.T, D_MODEL), F32),
        compiler_params=_cparams(),
        name="final_norm",
    )(x, g.reshape(1, -1))


def kernel(x_prompt, x_sample, state_ssd, state_gla, state_lru, c, c_ctx, ada_w, ada_b, norm_g, final_g, ssd_w_in, ssd_conv_w, ssd_conv_b, ssd_a_log, ssd_dt_bias, ssd_d, ssd_norm_g, ssd_w_out, gla_w_in, gla_w_gate_up, gla_b_gate, gla_norm_g, gla_w_out, lru_w_in, lru_conv_w, lru_conv_b, lru_w_a, lru_b_a, lru_w_x, lru_b_x, lru_lambda, lru_w_out, ffn_w1, ffn_w3, ffn_w2, moe_router, moe_w1, moe_w3, moe_w2):
    n_ctx, len_ctx, _ = x_prompt.shape
    n_lat, len_lat, _ = x_sample.shape
    lay = Layout(n_ctx, len_ctx, n_lat, len_lat)
    tl = _Tiles(lay, TM)
    assert len_ctx & (len_ctx - 1) == 0 and len_ctx <= TM and TM % GRID_W == 0

    x = jnp.concatenate([x_prompt.reshape(-1, D_MODEL), x_sample.reshape(-1, D_MODEL)], axis=0)
    c_all = jnp.concatenate([c_ctx[None], c], axis=0)
    c_all = jnp.pad(c_all, ((0, -c_all.shape[0] % SUBLANE), (0, 0)))
    mods = _mods(c_all, ada_w, ada_b)

    caches = (state_ssd, state_gla, state_lru)
    ctx_states = ([], [], [])
    for i in range(DEPTH):
        kind, j = i % N_MIXERS, i // N_MIXERS
        cache = caches[kind][:, j]
        h0 = jnp.concatenate([jnp.zeros((n_ctx,) + cache.shape[1:], F32), cache], axis=0)
        if kind == 0:
            x, hfin = _ssd_layer(tl, lay, x, mods[i], norm_g[i, 0], h0, ssd_w_in[j], ssd_conv_w[j],
                                 ssd_conv_b[j], ssd_a_log[j], ssd_dt_bias[j], ssd_d[j],
                                 ssd_norm_g[j], ssd_w_out[j])
        elif kind == 1:
            x, hfin = _gla_layer(tl, lay, x, mods[i], norm_g[i, 0], h0, gla_w_in[j], gla_w_gate_up[j],
                                 gla_b_gate[j], gla_norm_g[j], gla_w_out[j])
        else:
            x, hfin = _lru_layer(tl, lay, x, mods[i], norm_g[i, 0], h0, lru_w_in[j], lru_conv_w[j],
                                 lru_conv_b[j], lru_w_a[j], lru_b_a[j], lru_w_x[j], lru_b_x[j],
                                 lru_lambda[j], lru_w_out[j])
        ctx_states[kind].append(hfin[:n_ctx])
        if i % 2 == 0:
            x = _ffn_layer(tl, x, mods[i], norm_g[i, 1], ffn_w1[i // 2], ffn_w3[i // 2], ffn_w2[i // 2])
        else:
            x = _moe_layer(tl, x, mods[i], norm_g[i, 1], moe_router[i // 2], moe_w1[i // 2],
                           moe_w3[i // 2], moe_w2[i // 2])

    y = _final_norm(tl, x, final_g)
    n_c = n_ctx * len_ctx
    return (y[:n_c].reshape(n_ctx, len_ctx, D_MODEL),
            y[n_c:].reshape(n_lat, len_lat, D_MODEL),
            jnp.stack(ctx_states[0], axis=1),
            jnp.stack(ctx_states[1], axis=1),
            jnp.stack(ctx_states[2], axis=1))
```

```python
import collections
import functools

import jax
import jax.numpy as jnp
import numpy as np
from jax import lax
from jax.experimental import pallas as pl
from jax.experimental.pallas import tpu as pltpu

F32 = jnp.float32
BF16 = jnp.bfloat16
I32 = jnp.int32

D_MODEL = 1024
DEPTH = 4
GRID_W = 64
N_MIXERS = 3
EPS = 1e-6
CONV_W = 4

SSD_INNER = 2 * D_MODEL
SSD_HEAD_DIM = 64
SSD_HEADS = SSD_INNER // SSD_HEAD_DIM
SSD_GROUPS = 8
SSD_GROUP_HEADS = SSD_HEADS // SSD_GROUPS
SSD_GROUP_W = SSD_GROUP_HEADS * SSD_HEAD_DIM
SSD_STATE = 128
SSD_CHUNK = 128
SSD_BC = SSD_GROUPS * SSD_STATE
SSD_CONV_CH = SSD_INNER + 2 * SSD_BC

GLA_HEADS = 4
GLA_KEY = D_MODEL // 2
GLA_VAL = D_MODEL
GLA_DK = GLA_KEY // GLA_HEADS
GLA_DV = GLA_VAL // GLA_HEADS
GLA_RANK = 16
GLA_TAU = 16.0
GLA_CHUNK = 64

LRU_WIDTH = D_MODEL
LRU_BLOCKS = 8
LRU_BLOCK = LRU_WIDTH // LRU_BLOCKS
LRU_C = 8.0

D_FF = 2816
N_EXPERTS = 8

LANE = 128
SUBLANE = 8
TM = 256
TM_WIDE = 512
SCAN_BLK = 256
VMEM_LIMIT = 56 * 1024 * 1024

Layout = collections.namedtuple("Layout", "n_ctx len_ctx n_lat len_lat")


def _cparams(n_axes=1):
    return pltpu.CompilerParams(
        dimension_semantics=("arbitrary",) * n_axes, vmem_limit_bytes=VMEM_LIMIT)


def _silu(x):
    return x * jax.nn.sigmoid(x)


def _softplus(x):
    return jnp.maximum(x, 0.0) + jnp.log(1.0 + jnp.exp(-jnp.abs(x)))


def _rms(x, g):
    return x * lax.rsqrt(jnp.mean(x * x, axis=-1, keepdims=True) + EPS) * g


def _dot(a, b):
    return jnp.dot(a, b, preferred_element_type=F32)


def _dot_nt(a, b):
    return lax.dot_general(a, b, (((1,), (1,)), ((), ())), preferred_element_type=F32)


def _dot_tn(a, b):
    return lax.dot_general(a, b, (((0,), (0,)), ((), ())), preferred_element_type=F32)


def _split3(x):
    hi = x.astype(BF16)
    r1 = x - hi.astype(F32)
    mid = r1.astype(BF16)
    lo = (r1 - mid.astype(F32)).astype(BF16)
    return hi, mid, lo


def _tri_cumsum(tri, x):
    hi, mid, lo = _split3(x)
    return _dot(tri, hi) + _dot(tri, mid) + _dot(tri, lo)


class _Tiles:
    def __init__(self, lay, tm):
        assert (lay.n_ctx * lay.len_ctx) % tm == 0 and lay.len_lat % tm == 0
        self.lay = lay
        self.tm = tm
        self.nct = lay.n_ctx * lay.len_ctx // tm
        self.tpl = lay.len_lat // tm
        self.n = self.nct + lay.n_lat * self.tpl
        self.T = self.n * tm

    def mod_row(self, i):
        return jnp.where(i < self.nct, 0, 1 + (i - self.nct) // self.tpl)

    @property
    def wide(self):
        return _Tiles(self.lay, TM_WIDE)


def _full(shape):
    nd = len(shape)
    return pl.BlockSpec(shape, lambda *_: (0,) * nd, pipeline_mode=pl.Buffered(1))


def _mod_spec(tl):
    return pl.BlockSpec((None, 6, D_MODEL), lambda i, *_: (tl.mod_row(i), 0, 0))


def _row_spec(tl, width):
    return pl.BlockSpec((tl.tm, width), lambda i, *_: (i, 0))


def _mods_kernel(c_ref, w_ref, b_ref, o_ref):
    c = c_ref[...]
    a = _silu(c)
    o_ref[...] = jnp.dot(a, w_ref[...], preferred_element_type=F32,
                         precision=lax.Precision.HIGHEST) + b_ref[...]


def _mods(c_all, ada_w, ada_b):
    R = c_all.shape[0]
    nb = 6
    out = pl.pallas_call(
        _mods_kernel,
        grid=(DEPTH, nb),
        in_specs=[
            pl.BlockSpec((R, D_MODEL), lambda l, j: (0, 0)),
            pl.BlockSpec((None, D_MODEL, D_MODEL), lambda l, j: (l, 0, j)),
            pl.BlockSpec((None, 1, D_MODEL), lambda l, j: (l, 0, j)),
        ],
        out_specs=pl.BlockSpec((None, R, D_MODEL), lambda l, j: (l, 0, j)),
        out_shape=jax.ShapeDtypeStruct((DEPTH, R, 6 * D_MODEL), F32),
        compiler_params=_cparams(2),
        name="mods",
    )(c_all, ada_w, ada_b.reshape(DEPTH, 1, 6 * D_MODEL))
    return out.reshape(DEPTH, R, 6, D_MODEL)


CONV_OFFSETS = (-1, 0, 1, 2)


def _shift_mats(tm, segs):
    t = np.arange(tm)
    out = []
    for seg in segs:
        mats = []
        for off in CONV_OFFSETS:
            if off == 0:
                continue
            src = t + off
            ok = (src >= 0) & (src < tm) & (src // seg == t // seg)
            m = np.zeros((tm, tm), np.float32)
            m[t[ok], src[ok]] = 1.0
            mats.append(m)
        out.append(np.concatenate(mats, axis=1))
    return jnp.asarray(np.stack(out), BF16)


def _shift_spec(tl):
    return pl.BlockSpec((None, tl.tm, 3 * tl.tm), lambda i: (jnp.where(i < tl.nct, 0, 1), 0, 0))


def _conv4(p, cw, cb, shifts):
    taps = [(p * cw[k:k + 1]).astype(BF16) for k in (0, 2, 3)]
    return cb + cw[1:2] * p + _dot(shifts, jnp.concatenate(taps, axis=0))


def _ssd_in_kernel(x_ref, mod_ref, g_ref, w_ref, wdt_ref, cw_ref, cb_ref, dtb_ref, sh_ref,
                   z_ref, xs_ref, b_ref, c_ref, dt_ref):
    mod = mod_ref[...]
    u = (_rms(x_ref[...], g_ref[...]) * (1.0 + mod[1:2]) + mod[0:1]).astype(BF16)
    z_ref[...] = _dot(u, w_ref[:, :SSD_INNER]).astype(z_ref.dtype)
    shifts = sh_ref[...]
    ch = 512
    for c0 in range(0, SSD_CONV_CH, ch):
        p = _dot(u, w_ref[:, SSD_INNER + c0:SSD_INNER + c0 + ch])
        y = _silu(_conv4(p, cw_ref[:, c0:c0 + ch], cb_ref[:, c0:c0 + ch], shifts))
        if c0 < SSD_INNER:
            dst, base, w = xs_ref, c0, SSD_GROUP_W
        elif c0 < SSD_INNER + SSD_BC:
            dst, base, w = b_ref, c0 - SSD_INNER, SSD_STATE
        else:
            dst, base, w = c_ref, c0 - SSD_INNER - SSD_BC, SSD_STATE
        for k in range(ch // w):
            dst[base // w + k] = y[:, k * w:(k + 1) * w].astype(dst.dtype)
    dt_ref[...] = _softplus(_dot(u, wdt_ref[...]) + dtb_ref[...])


def _ssd_in(tl, lay, x, mod, g, w_main, w_dt, conv_w, conv_b, dt_bias):
    T = tl.T
    gspec = lambda w: pl.BlockSpec((SSD_GROUPS, tl.tm, w), lambda i: (0, i, 0))
    return pl.pallas_call(
        _ssd_in_kernel,
        grid=(tl.n,),
        in_specs=[
            _row_spec(tl, D_MODEL), _mod_spec(tl), _full((1, D_MODEL)),
            _full(w_main.shape), _full(w_dt.shape), _full(conv_w.shape),
            _full(conv_b.shape), _full(dt_bias.shape), _shift_spec(tl),
        ],
        out_specs=[
            _row_spec(tl, SSD_INNER), gspec(SSD_GROUP_W), gspec(SSD_STATE), gspec(SSD_STATE),
            _row_spec(tl, 2 * LANE),
        ],
        out_shape=[
            jax.ShapeDtypeStruct((T, SSD_INNER), BF16),
            jax.ShapeDtypeStruct((SSD_GROUPS, T, SSD_GROUP_W), BF16),
            jax.ShapeDtypeStruct((SSD_GROUPS, T, SSD_STATE), BF16),
            jax.ShapeDtypeStruct((SSD_GROUPS, T, SSD_STATE), BF16),
            jax.ShapeDtypeStruct((T, 2 * LANE), F32),
        ],
        compiler_params=_cparams(),
        name="ssd_in",
    )(x, mod, g, w_main, w_dt, conv_w, conv_b, dt_bias, _shift_mats(tl.tm, (lay.len_ctx, GRID_W)))


def _ssd_scan_kernel(bc_ref, seq_ref, first_ref, last_ref,
                     xf_ref, bf_ref, cf_ref, dtf_ref, xb_ref, bb_ref, cb_ref, dtb_ref,
                     alog_ref, h0_ref, yf_ref, yb_ref, hout_ref, h_s, cum_s, crow_s, wrow_s):
    i = pl.program_id(0)
    Q = SSD_CHUNK
    P = SSD_HEAD_DIM
    pairs_per_group = SSD_GROUP_HEADS // 2

    @pl.when(first_ref[i] == 1)
    def _():
        h_s[...] = h0_ref[...]

    rows = lax.broadcasted_iota(I32, (Q, Q), 0)
    cols = lax.broadcasted_iota(I32, (Q, Q), 1)
    masks = (rows >= cols, rows <= cols)
    first_head = lax.broadcasted_iota(I32, (1, LANE), 1) < P
    dirs = ((xf_ref, bf_ref, cf_ref, dtf_ref, yf_ref), (xb_ref, bb_ref, cb_ref, dtb_ref, yb_ref))

    for d in range(2):
        x_ref, b_ref, c_ref, dt_ref, y_ref = dirs[d]
        end = Q - 1 if d == 0 else 0
        dt = dt_ref[...]
        cum = _tri_cumsum(masks[d].astype(BF16), -dt * jnp.exp(alog_ref[d:d + 1]))
        cum_t = cum.T
        dt_t = dt.T
        cum_s[d] = cum
        crow_s[d] = cum_t - jnp.log(dt_t)
        wrow_s[d] = dt_t * jnp.exp(cum_t[:, end:end + 1] - cum_t)
        for g in range(SSD_GROUPS):
            bg = b_ref[g]
            cg = c_ref[g]
            cb = _dot_nt(cg, bg)
            cgf = cg.astype(F32)
            bgt = bg.astype(F32).T
            for j in range(pairs_per_group):
                p = g * pairs_per_group + j
                xp = x_ref[g, :, j * LANE:(j + 1) * LANE]
                zero = jnp.zeros_like(xp)
                xa = jnp.where(first_head, xp, zero)
                xb = jnp.where(first_head, zero, xp)
                hp = h_s[d, p]
                hpb = hp.astype(BF16)
                ha = jnp.where(first_head, hpb, jnp.zeros_like(hpb))
                hb = jnp.where(first_head, jnp.zeros_like(hpb), hpb)
                lhs, bws, tots = [], [], []
                for h in (2 * p, 2 * p + 1):
                    ccol = jnp.broadcast_to(cum_s[d, :, h:h + 1], (Q, Q))
                    ecol = jnp.exp(ccol)
                    dec = jnp.where(masks[d], jnp.exp(ccol - crow_s[d, h:h + 1, :]), 0.0)
                    lhs += [(cb * dec).astype(BF16), (cgf * ecol).astype(BF16)]
                    bws.append((bgt * wrow_s[d, h:h + 1, :]).astype(BF16))
                    tots.append(ecol[end:end + 1])
                y = _dot(jnp.concatenate(lhs, axis=1), jnp.concatenate([xa, ha, xb, hb], axis=0))
                y_ref[g, :, j * LANE:(j + 1) * LANE] = y.astype(y_ref.dtype)
                upd = _dot(jnp.concatenate(bws, axis=1), jnp.concatenate([xa, xb], axis=0))
                h_s[d, p] = hp * jnp.where(first_head, tots[0], tots[1]) + upd

    @pl.when(last_ref[i] == 1)
    def _():
        hout_ref[...] = h_s[...]


def _scan_tables(lay, blk):
    bc, seq, first, last = [], [], [], []
    s0 = 0
    sid = 0
    for n, L in ((lay.n_ctx, lay.len_ctx), (lay.n_lat, lay.len_lat)):
        nb = L // blk
        for _ in range(n):
            for k in range(nb):
                bc.append(s0 + nb - 1 - k)
                seq.append(sid)
                first.append(int(k == 0))
                last.append(int(k == nb - 1))
            s0 += nb
            sid += 1
    return tuple(jnp.asarray(np.asarray(a, np.int32)) for a in (bc, seq, first, last))


def _ssd_scan(lay, xs, bm, cm, dt, a_log, h0):
    Q = SSD_CHUNK
    T = xs.shape[1]
    tabs = _scan_tables(lay, Q)
    nseq = lay.n_ctx + lay.n_lat
    G, N, W = SSD_GROUPS, SSD_STATE, SSD_GROUP_W
    st_shape = (2, SSD_HEADS // 2, N, 2 * SSD_HEAD_DIM)
    assert 2 * SSD_HEAD_DIM == LANE and Q == LANE
    fwd = lambda w: pl.BlockSpec((G, Q, w), lambda i, bc, sq, fi, la: (0, i, 0))
    bwd = lambda w: pl.BlockSpec((G, Q, w), lambda i, bc, sq, fi, la: (0, bc[i], 0))
    st_spec = pl.BlockSpec((None,) + st_shape, lambda i, bc, sq, fi, la: (sq[i], 0, 0, 0, 0))
    grid_spec = pltpu.PrefetchScalarGridSpec(
        num_scalar_prefetch=4,
        grid=(T // Q,),
        in_specs=[
            fwd(W), fwd(N), fwd(N), pl.BlockSpec((Q, LANE), lambda i, bc, sq, fi, la: (i, 0)),
            bwd(W), bwd(N), bwd(N), pl.BlockSpec((Q, LANE), lambda i, bc, sq, fi, la: (bc[i], 1)),
            pl.BlockSpec((2, LANE), lambda i, *_: (0, 0)),
            st_spec,
        ],
        out_specs=[fwd(W), bwd(W), st_spec],
        scratch_shapes=[pltpu.VMEM(st_shape, F32), pltpu.VMEM((2, Q, LANE), F32),
                        pltpu.VMEM((2, LANE, Q), F32), pltpu.VMEM((2, LANE, Q), F32)],
    )
    return pl.pallas_call(
        _ssd_scan_kernel,
        grid_spec=grid_spec,
        out_shape=[
            jax.ShapeDtypeStruct((G, T, W), BF16),
            jax.ShapeDtypeStruct((G, T, W), BF16),
            jax.ShapeDtypeStruct((nseq,) + st_shape, F32),
        ],
        compiler_params=_cparams(),
        name="ssd_scan",
    )(*tabs, xs, bm, cm, dt, xs, bm, cm, dt, a_log, h0)


def _ssd_out_kernel(yf_ref, yb_ref, xs_ref, z_ref, x_ref, mod_ref, d_ref, ng_ref, w_ref, o_ref):
    parts = []
    for g in range(SSD_GROUPS):
        dg = d_ref[:, g * SSD_GROUP_W:(g + 1) * SSD_GROUP_W]
        parts.append(yf_ref[g].astype(F32) + yb_ref[g].astype(F32) + dg * xs_ref[g].astype(F32))
    y = jnp.concatenate(parts, axis=1)
    y = _rms(y * _silu(z_ref[...].astype(F32)), ng_ref[...]).astype(BF16)
    o_ref[...] = x_ref[...] + mod_ref[2:3] * _dot(y, w_ref[...])


def _ssd_out(tl, yf, yb, xs, z, x, mod, d_exp, norm_g, w_out):
    tl = tl.wide
    gspec =pl.BlockSpec((SSD_GROUPS, tl.tm, SSD_GROUP_W), lambda i: (0, i, 0))
    return pl.pallas_call(
        _ssd_out_kernel,
        grid=(tl.n,),
        in_specs=[gspec, gspec, gspec, _row_spec(tl, SSD_INNER), _row_spec(tl, D_MODEL),
                  _mod_spec(tl), _full(d_exp.shape), _full(norm_g.shape), _full(w_out.shape)],
        out_specs=_row_spec(tl, D_MODEL),
        out_shape=jax.ShapeDtypeStruct((tl.T, D_MODEL), F32),
        compiler_params=_cparams(),
        name="ssd_out",
    )(yf, yb, xs, z, x, mod, d_exp, norm_g, w_out)


def _ssd_layer(tl, lay, x, mod, g, h0, w_in, conv_w, conv_b, a_log, dt_bias, d_skip, norm_g, w_out):
    nseq = h0.shape[0]
    split = SSD_INNER + SSD_CONV_CH
    w_main = w_in[:, :split].astype(BF16)
    lane_pad = lambda v: jnp.pad(v, ((0, 0), (0, 0), (0, LANE - SSD_HEADS)))
    w_dt = lane_pad(w_in[:, split:].reshape(D_MODEL, 2, SSD_HEADS)).reshape(D_MODEL, 2 * LANE).astype(BF16)
    dtb = lane_pad(dt_bias.reshape(1, 2, SSD_HEADS)).reshape(1, 2 * LANE)
    alog = lane_pad(a_log.reshape(1, 2, SSD_HEADS)).reshape(2, LANE)
    z, xs, bm, cm, dt = _ssd_in(tl, lay, x, mod, g.reshape(1, -1), w_main, w_dt,
                                conv_w, conv_b.reshape(1, -1), dtb)
    pair_shape = (nseq, 2, SSD_HEADS // 2, 2 * SSD_HEAD_DIM, SSD_STATE)
    yf, yb, hfin = _ssd_scan(lay, xs, bm, cm, dt, alog, h0.reshape(pair_shape).swapaxes(-1, -2))
    hfin = hfin.swapaxes(-1, -2).reshape(h0.shape)
    d_exp = jnp.repeat(d_skip, SSD_HEAD_DIM).reshape(1, -1)
    x = _ssd_out(tl, yf, yb, xs, z, x, mod, d_exp, norm_g.reshape(1, -1), w_out.astype(BF16))
    return x, hfin


def _gla_in_kernel(x_ref, mod_ref, g_ref, w_ref, wgd_ref, wup_ref, bg_ref,
                   q_ref, k_ref, v_ref, r_ref, la_ref):
    mod = mod_ref[...]
    u = (_rms(x_ref[...], g_ref[...]) * (1.0 + mod[1:2]) + mod[0:1]).astype(BF16)
    q_ref[...] = (_dot(u, w_ref[:, :GLA_KEY]) * (GLA_DK ** -0.5)).astype(q_ref.dtype)
    k_ref[...] = _dot(u, w_ref[:, GLA_KEY:2 * GLA_KEY]).astype(k_ref.dtype)
    v_ref[...] = _dot(u, w_ref[:, 2 * GLA_KEY:2 * GLA_KEY + GLA_VAL]).astype(v_ref.dtype)
    r_ref[...] = _dot(u, w_ref[:, 2 * GLA_KEY + GLA_VAL:]).astype(r_ref.dtype)
    gd = _dot(u, wgd_ref[...]).astype(BF16)
    logit = _dot(gd, wup_ref[...]) + bg_ref[...]
    la_ref[...] = (jnp.minimum(logit, 0.0) - jnp.log(1.0 + jnp.exp(-jnp.abs(logit)))) / GLA_TAU


def _gla_in(tl, x, mod, g, w_main, w_gd, w_up, b_gate):
    tl = tl.wide
    T = tl.T
    return pl.pallas_call(
        _gla_in_kernel,
        grid=(tl.n,),
        in_specs=[_row_spec(tl, D_MODEL), _mod_spec(tl), _full((1, D_MODEL)), _full(w_main.shape),
                  _full(w_gd.shape), _full(w_up.shape), _full(b_gate.shape)],
        out_specs=[_row_spec(tl, GLA_KEY), _row_spec(tl, GLA_KEY), _row_spec(tl, GLA_VAL),
                   _row_spec(tl, GLA_VAL), _row_spec(tl, 2 * GLA_KEY)],
        out_shape=[
            jax.ShapeDtypeStruct((T, GLA_KEY), BF16),
            jax.ShapeDtypeStruct((T, GLA_KEY), BF16),
            jax.ShapeDtypeStruct((T, GLA_VAL), BF16),
            jax.ShapeDtypeStruct((T, GLA_VAL), BF16),
            jax.ShapeDtypeStruct((T, 2 * GLA_KEY), F32),
        ],
        compiler_params=_cparams(),
        name="gla_in",
    )(x, mod, g, w_main, w_gd, w_up, b_gate)


def _gla_scan_kernel(bc_ref, seq_ref, first_ref, last_ref,
                     qf_ref, kf_ref, vf_ref, laf_ref, qb_ref, kb_ref, vb_ref, lab_ref, s0_ref,
                     of_ref, ob_ref, sout_ref, s_s):
    i = pl.program_id(0)
    C = GLA_CHUNK
    nsub = qf_ref.shape[0] // C

    @pl.when(first_ref[i] == 1)
    def _():
        s_s[...] = s0_ref[...]

    nrow = qf_ref.shape[0]
    rows = lax.broadcasted_iota(I32, (nrow, nrow), 0)
    cols = lax.broadcasted_iota(I32, (nrow, nrow), 1)
    shift = C.bit_length() - 1
    assert C == 1 << shift
    same_chunk = jnp.right_shift(rows, shift) == jnp.right_shift(cols, shift)
    tris = (jnp.where(same_chunk & (rows >= cols), 1.0, 0.0).astype(BF16),
            jnp.where(same_chunk & (rows <= cols), 1.0, 0.0).astype(BF16))
    rows_c = lax.broadcasted_iota(I32, (C, C), 0)
    cols_c = lax.broadcasted_iota(I32, (C, C), 1)
    masks = (rows_c >= cols_c, rows_c <= cols_c)
    dirs = ((qf_ref, kf_ref, vf_ref, laf_ref, of_ref), (qb_ref, kb_ref, vb_ref, lab_ref, ob_ref))

    for d in range(2):
        q_ref, k_ref, v_ref, la_ref, o_ref = dirs[d]
        gcum_all = _tri_cumsum(tris[d], la_ref[:, d * GLA_KEY:(d + 1) * GLA_KEY])
        for c in range(nsub):
            cc = c if d == 0 else nsub - 1 - c
            rs = slice(cc * C, (cc + 1) * C)
            for h in range(GLA_HEADS):
                ks = slice(h * GLA_DK, (h + 1) * GLA_DK)
                vs = slice(h * GLA_DV, (h + 1) * GLA_DV)
                gcum = gcum_all[rs, ks]
                qh = q_ref[rs, ks].astype(F32)
                kh = k_ref[rs, ks].astype(F32)
                vh = v_ref[rs, vs]
                qi = (qh * jnp.exp(gcum)).astype(BF16)
                ki = (kh * jnp.exp(-gcum)).astype(BF16)
                att = jnp.where(masks[d], _dot_nt(qi, ki), 0.0).astype(BF16)
                st = s_s[d, h]
                o_ref[rs, vs] = (_dot(att, vh) + _dot_nt(qi, st.astype(BF16))).astype(o_ref.dtype)
                glast = gcum[C - 1:C] if d == 0 else gcum[0:1]
                kdec = (kh * jnp.exp(glast - gcum)).astype(BF16)
                s_s[d, h] = st * jnp.exp(glast) + _dot_tn(vh, kdec)

    @pl.when(last_ref[i] == 1)
    def _():
        sout_ref[...] = s_s[...]


def _gla_scan(lay, q, k, v, la, s0):
    B = SCAN_BLK
    T = q.shape[0]
    tabs = _scan_tables(lay, B)
    nseq = lay.n_ctx + lay.n_lat
    fwd = lambda w: pl.BlockSpec((B, w), lambda i, bc, sq, fi, la_: (i, 0))
    bwd = lambda w: pl.BlockSpec((B, w), lambda i, bc, sq, fi, la_: (bc[i], 0))
    st_spec = pl.BlockSpec((None, 2, GLA_HEADS, GLA_DV, GLA_DK),
                           lambda i, bc, sq, fi, la_: (sq[i], 0, 0, 0, 0))
    grid_spec = pltpu.PrefetchScalarGridSpec(
        num_scalar_prefetch=4,
        grid=(T // B,),
        in_specs=[fwd(GLA_KEY), fwd(GLA_KEY), fwd(GLA_VAL), fwd(2 * GLA_KEY),
                  bwd(GLA_KEY), bwd(GLA_KEY), bwd(GLA_VAL), bwd(2 * GLA_KEY), st_spec],
        out_specs=[fwd(GLA_VAL), bwd(GLA_VAL), st_spec],
        scratch_shapes=[pltpu.VMEM((2, GLA_HEADS, GLA_DV, GLA_DK), F32)],
    )
    return pl.pallas_call(
        _gla_scan_kernel,
        grid_spec=grid_spec,
        out_shape=[
            jax.ShapeDtypeStruct((T, GLA_VAL), BF16),
            jax.ShapeDtypeStruct((T, GLA_VAL), BF16),
            jax.ShapeDtypeStruct((nseq, 2, GLA_HEADS, GLA_DV, GLA_DK), F32),
        ],
        compiler_params=_cparams(),
        name="gla_scan",
    )(*tabs, q, k, v, la, q, k, v, la, s0)


def _gla_out_kernel(of_ref, ob_ref, r_ref, x_ref, mod_ref, ng_ref, w_ref, o_ref):
    o = of_ref[...].astype(F32) + ob_ref[...].astype(F32)
    ng = ng_ref[...]
    parts = [_rms(o[:, h * GLA_DV:(h + 1) * GLA_DV], ng) for h in range(GLA_HEADS)]
    y = (jnp.concatenate(parts, axis=1) * _silu(r_ref[...].astype(F32))).astype(BF16)
    o_ref[...] = x_ref[...] + mod_ref[2:3] * _dot(y, w_ref[...])


def _gla_out(tl, of, ob, r, x, mod, norm_g, w_out):
    tl = tl.wide
    return pl.pallas_call(
        _gla_out_kernel,
        grid=(tl.n,),
        in_specs=[_row_spec(tl, GLA_VAL), _row_spec(tl, GLA_VAL), _row_spec(tl, GLA_VAL),
                  _row_spec(tl, D_MODEL), _mod_spec(tl), _full(norm_g.shape), _full(w_out.shape)],
        out_specs=_row_spec(tl, D_MODEL),
        out_shape=jax.ShapeDtypeStruct((tl.T, D_MODEL), F32),
        compiler_params=_cparams(),
        name="gla_out",
    )(of, ob, r, x, mod, norm_g, w_out)


def _gla_layer(tl, lay, x, mod, g, s0, w_in, w_gate_up, b_gate, norm_g, w_out):
    split = 2 * GLA_KEY + 2 * GLA_VAL
    w_main = w_in[:, :split].astype(BF16)
    w_gd = jnp.pad(w_in[:, split:], ((0, 0), (0, LANE - 2 * GLA_RANK))).astype(BF16)
    w_up = jnp.zeros((LANE, 2 * GLA_KEY), F32)
    for d in range(2):
        w_up = w_up.at[d * GLA_RANK:(d + 1) * GLA_RANK, d * GLA_KEY:(d + 1) * GLA_KEY].set(w_gate_up[d])
    q, k, v, r, la = _gla_in(tl, x, mod, g.reshape(1, -1), w_main, w_gd, w_up.astype(BF16),
                             b_gate.reshape(1, -1))
    of, ob, sfin = _gla_scan(lay, q, k, v, la, s0.swapaxes(-1, -2))
    x = _gla_out(tl, of, ob, r, x, mod, norm_g.reshape(1, -1), w_out.astype(BF16))
    return x, sfin.swapaxes(-1, -2)


def _gelu_tanh(x):
    return 0.5 * x * (1.0 + jnp.tanh(0.7978845608028654 * (x + 0.044715 * (x * x * x))))


def _lru_in_kernel(x_ref, mod_ref, g_ref, w_ref, cw_ref, cb_ref, wg_ref, bg_ref, lam_ref, sh_ref,
                   a_ref, b_ref, gg_ref):
    W = LRU_WIDTH
    mod = mod_ref[...]
    u = (_rms(x_ref[...], g_ref[...]) * (1.0 + mod[1:2]) + mod[0:1]).astype(BF16)
    gg_ref[...] = _gelu_tanh(_dot(u, w_ref[:, :W])).astype(gg_ref.dtype)
    xb = _conv4(_dot(u, w_ref[:, W:]), cw_ref[...], cb_ref[...], sh_ref[...])
    sp = _softplus(-lam_ref[...])
    for n in range(LRU_BLOCKS):
        cs = slice(n * LRU_BLOCK, (n + 1) * LRU_BLOCK)
        xn = xb[:, cs]
        gates = jax.nn.sigmoid(_dot(xn.astype(BF16), wg_ref[n]) + bg_ref[n])
        for d in range(2):
            r = gates[:, d * LRU_BLOCK:(d + 1) * LRU_BLOCK]
            ig = gates[:, (2 + d) * LRU_BLOCK:(3 + d) * LRU_BLOCK]
            log_a = (-LRU_C) * r * sp[:, d * W + n * LRU_BLOCK:d * W + (n + 1) * LRU_BLOCK]
            a = jnp.exp(log_a)
            ds = slice(d * W + n * LRU_BLOCK, d * W + (n + 1) * LRU_BLOCK)
            a_ref[:, ds] = a
            b_ref[:, ds] = jnp.sqrt(1.0 - a * a) * (ig * xn)


def _lru_in(tl, lay, x, mod, g, w_in, conv_w, conv_b, w_gates, b_gates, lam):
    T = tl.T
    return pl.pallas_call(
        _lru_in_kernel,
        grid=(tl.n,),
        in_specs=[_row_spec(tl, D_MODEL), _mod_spec(tl), _full((1, D_MODEL)), _full(w_in.shape),
                  _full(conv_w.shape), _full(conv_b.shape), _full(w_gates.shape),
                  _full(b_gates.shape), _full(lam.shape), _shift_spec(tl)],
        out_specs=[_row_spec(tl, 2 * LRU_WIDTH), _row_spec(tl, 2 * LRU_WIDTH), _row_spec(tl, LRU_WIDTH)],
        out_shape=[
            jax.ShapeDtypeStruct((T, 2 * LRU_WIDTH), F32),
            jax.ShapeDtypeStruct((T, 2 * LRU_WIDTH), F32),
            jax.ShapeDtypeStruct((T, LRU_WIDTH), BF16),
        ],
        compiler_params=_cparams(),
        name="lru_in",
    )(x, mod, g, w_in, conv_w, conv_b, w_gates, b_gates, lam, _shift_mats(tl.tm, (lay.len_ctx, GRID_W)))


def _lru_scan_kernel(bc_ref, seq_ref, first_ref, last_ref,
                     af_ref, bf_ref, ab_ref, bb_ref, h0_ref, hf_ref, hb_ref, hout_ref, h_s):
    i = pl.program_id(0)
    S = SUBLANE
    W = LRU_WIDTH
    ntile = af_ref.shape[0] // S

    @pl.when(first_ref[i] == 1)
    def _():
        h_s[...] = h0_ref[...]

    ri = lax.broadcasted_iota(I32, (S, W), 0)

    def tile(j, carry):
        hf, hb = carry
        rs = pl.ds(pl.multiple_of(j * S, S), S)
        a = af_ref[rs, 0:W]
        b = bf_ref[rs, 0:W]
        for sh in (1, 2, 4):
            keep = ri >= sh
            a_s = jnp.where(keep, pltpu.roll(a, sh, 0), 1.0)
            b_s = jnp.where(keep, pltpu.roll(b, sh, 0), 0.0)
            b = a * b_s + b
            a = a * a_s
        h8 = a * hf + b
        hf_ref[rs, :] = h8
        hf = h8[S - 1:S]
        rs = pl.ds(pl.multiple_of((ntile - 1 - j) * S, S), S)
        a = ab_ref[rs, W:2 * W]
        b = bb_ref[rs, W:2 * W]
        for sh in (1, 2, 4):
            keep = ri < S - sh
            a_s = jnp.where(keep, pltpu.roll(a, S - sh, 0), 1.0)
            b_s = jnp.where(keep, pltpu.roll(b, S - sh, 0), 0.0)
            b = a * b_s + b
            a = a * a_s
        h8 = a * hb + b
        hb_ref[rs, :] = h8
        hb = h8[0:1]
        return hf, hb

    hf, hb = lax.fori_loop(0, ntile, tile, (h_s[0:1], h_s[1:2]))
    h_s[0:1] = hf
    h_s[1:2] = hb

    @pl.when(last_ref[i] == 1)
    def _():
        hout_ref[...] = h_s[...]


def _lru_scan(lay, a, b, h0):
    B = SCAN_BLK
    T = a.shape[0]
    W = LRU_WIDTH
    tabs = _scan_tables(lay, B)
    nseq = lay.n_ctx + lay.n_lat
    fwd = lambda w: pl.BlockSpec((B, w), lambda i, bc, sq, fi, la_: (i, 0))
    bwd = lambda w: pl.BlockSpec((B, w), lambda i, bc, sq, fi, la_: (bc[i], 0))
    st_spec = pl.BlockSpec((None, 2, W), lambda i, bc, sq, fi, la_: (sq[i], 0, 0))
    grid_spec = pltpu.PrefetchScalarGridSpec(
        num_scalar_prefetch=4,
        grid=(T // B,),
        in_specs=[fwd(2 * W), fwd(2 * W), bwd(2 * W), bwd(2 * W), st_spec],
        out_specs=[fwd(W), bwd(W), st_spec],
        scratch_shapes=[pltpu.VMEM((2, W), F32)],
    )
    return pl.pallas_call(
        _lru_scan_kernel,
        grid_spec=grid_spec,
        out_shape=[
            jax.ShapeDtypeStruct((T, W), F32),
            jax.ShapeDtypeStruct((T, W), F32),
            jax.ShapeDtypeStruct((nseq, 2, W), F32),
        ],
        compiler_params=_cparams(),
        name="lru_scan",
    )(*tabs, a, b, a, b, h0)


def _lru_out_kernel(hf_ref, hb_ref, gg_ref, x_ref, mod_ref, w_ref, o_ref):
    y = ((hf_ref[...] + hb_ref[...]) * gg_ref[...].astype(F32)).astype(BF16)
    o_ref[...] = x_ref[...] + mod_ref[2:3] * _dot(y, w_ref[...])


def _lru_out(tl, hf, hb, gg, x, mod, w_out):
    tl = tl.wide
    W = LRU_WIDTH
    return pl.pallas_call(
        _lru_out_kernel,
        grid=(tl.n,),
        in_specs=[_row_spec(tl, W), _row_spec(tl, W), _row_spec(tl, W), _row_spec(tl, D_MODEL),
                  _mod_spec(tl), _full(w_out.shape)],
        out_specs=_row_spec(tl, D_MODEL),
        out_shape=jax.ShapeDtypeStruct((tl.T, D_MODEL), F32),
        compiler_params=_cparams(),
        name="lru_out",
    )(hf, hb, gg, x, mod, w_out)


def _lru_layer(tl, lay, x, mod, g, h0, w_in, conv_w, conv_b, w_a, b_a, w_x, b_x, lam, w_out):
    w_gates = jnp.concatenate([w_a[0], w_a[1], w_x[0], w_x[1]], axis=-1).astype(BF16)
    blk = lambda v: v.reshape(2, LRU_BLOCKS, LRU_BLOCK)
    b_gates = jnp.concatenate([blk(b_a)[0], blk(b_a)[1], blk(b_x)[0], blk(b_x)[1]], axis=-1)
    b_gates = b_gates.reshape(LRU_BLOCKS, 1, 4 * LRU_BLOCK)
    a, b, gg = _lru_in(tl, lay, x, mod, g.reshape(1, -1), w_in.astype(BF16), conv_w,
                       conv_b.reshape(1, -1), w_gates, b_gates, lam.reshape(1, -1))
    hf, hb, hfin = _lru_scan(lay, a, b, h0)
    x = _lru_out(tl, hf, hb, gg, x, mod, w_out.astype(BF16))
    return x, hfin


def _swiglu(u, w1_ref, w3_ref, w2_ref):
    a = _dot(u, w1_ref[...])
    b = _dot(u, w3_ref[...])
    return _dot((_silu(a) * b).astype(BF16), w2_ref[...])


def _ffn_kernel(x_ref, mod_ref, g_ref, w1_ref, w3_ref, w2_ref, o_ref):
    x = x_ref[...]
    mod = mod_ref[...]
    u = (_rms(x, g_ref[...]) * (1.0 + mod[4:5]) + mod[3:4]).astype(BF16)
    o_ref[...] = x + mod[5:6] * _swiglu(u, w1_ref, w3_ref, w2_ref)


def _ffn_layer(tl, x, mod, g, w1, w3, w2):
    tl = tl.wide
    w1, w3, w2 = w1.astype(BF16), w3.astype(BF16), w2.astype(BF16)
    return pl.pallas_call(
        _ffn_kernel,
        grid=(tl.n,),
        in_specs=[_row_spec(tl, D_MODEL), _mod_spec(tl), _full((1, D_MODEL)),
                  _full(w1.shape), _full(w3.shape), _full(w2.shape)],
        out_specs=_row_spec(tl, D_MODEL),
        out_shape=jax.ShapeDtypeStruct((tl.T, D_MODEL), F32),
        compiler_params=_cparams(),
        name="ffn",
    )(x, mod, g.reshape(1, -1), w1, w3, w2)


SLAB = D_MODEL // LANE
assert SLAB == SUBLANE


def _to_slab(ref, val):
    tm = val.shape[0]
    for s in range(SLAB):
        ref[pl.ds(s, tm, stride=SLAB), :] = val[:, s * LANE:(s + 1) * LANE]


def _from_slab(ref, tm):
    return jnp.concatenate([ref[pl.ds(s, tm, stride=SLAB), :] for s in range(SLAB)], axis=1)


def _moe_route_kernel(x_ref, mod_ref, g_ref, wr_ref, u_ref, mi_ref, mf_ref, cnt_ref, cnt_s):
    i = pl.program_id(0)
    tm = x_ref.shape[0]

    @pl.when(i == 0)
    def _():
        cnt_s[...] = jnp.zeros_like(cnt_s)

    mod = mod_ref[...]
    u = _rms(x_ref[...], g_ref[...]) * (1.0 + mod[4:5]) + mod[3:4]
    _to_slab(u_ref, u)
    logits = jnp.dot(u, wr_ref[...], preferred_element_type=F32, precision=lax.Precision.HIGHEST)
    lane = lax.broadcasted_iota(I32, (tm, LANE), 1)
    lg = jnp.where(lane < N_EXPERTS, logits, -jnp.inf)
    m1 = jnp.max(lg, axis=1, keepdims=True)
    i1 = jnp.min(jnp.where(lg == m1, lane, LANE), axis=1, keepdims=True)
    lg2 = jnp.where(lane == i1, -jnp.inf, lg)
    m2 = jnp.max(lg2, axis=1, keepdims=True)
    i2 = jnp.min(jnp.where(lg2 == m2, lane, LANE), axis=1, keepdims=True)
    e2 = jnp.exp(m2 - m1)
    g1 = 1.0 / (1.0 + e2)
    g2 = e2 / (1.0 + e2)
    oh1 = lane == i1
    oh2 = lane == i2
    cnt = (oh1 | oh2).astype(BF16)
    rr = lax.broadcasted_iota(I32, (tm, tm), 0)
    cc = lax.broadcasted_iota(I32, (tm, tm), 1)
    before = _dot((rr > cc).astype(BF16), cnt) + cnt_s[...]
    r1 = jnp.sum(jnp.where(oh1, before, 0.0), axis=1, keepdims=True).astype(I32)
    r2 = jnp.sum(jnp.where(oh2, before, 0.0), axis=1, keepdims=True).astype(I32)
    cnt_s[...] = cnt_s[...] + jnp.sum(cnt.astype(F32), axis=0, keepdims=True)
    mi_ref[...] = jnp.where(lane == 0, i1, jnp.where(lane == 1, i2, jnp.where(lane == 2, r1, r2)))
    mf_ref[...] = jnp.where(lane == 0, g1, g2)
    cnt_ref[...] = cnt_s[...]


def _moe_route(tl, x, mod, g, w_router):
    T = tl.T
    return pl.pallas_call(
        _moe_route_kernel,
        grid=(tl.n,),
        in_specs=[_row_spec(tl, D_MODEL), _mod_spec(tl), _full((1, D_MODEL)), _full(w_router.shape)],
        out_specs=[pl.BlockSpec((tl.tm * SLAB, LANE), lambda i: (i, 0)),
                   _row_spec(tl, LANE), _row_spec(tl, LANE),
                   pl.BlockSpec((1, LANE), lambda i: (0, 0))],
        out_shape=[
            jax.ShapeDtypeStruct((T * SLAB, LANE), F32),
            jax.ShapeDtypeStruct((T, LANE), I32),
            jax.ShapeDtypeStruct((T, LANE), F32),
            jax.ShapeDtypeStruct((1, LANE), F32),
        ],
        scratch_shapes=[pltpu.VMEM((1, LANE), F32)],
        compiler_params=_cparams(),
        name="moe_route",
    )(x, mod, g, w_router)


DISPATCH_BLK = 512


def _moe_dispatch_kernel(pos_ref, tail_ref, u_ref, z_hbm, xs_hbm, sem, *, nblk, tm, nt):
    i = pl.program_id(0)

    def rows(t, n=1):
        return pl.ds(pl.multiple_of(t * SLAB, SLAB), n * SLAB)

    def issue(r, carry):
        for k in range(2):
            pltpu.make_async_copy(u_ref.at[rows(r)], xs_hbm.at[rows(pos_ref[2 * r + k])],
                                  sem).start(priority=k)
        return carry

    lax.fori_loop(0, nblk, issue, 0, unroll=8)
    for k in range(2):
        pltpu.make_async_copy(u_ref, xs_hbm.at[rows(0, nblk)], sem).wait()

    @pl.when(i == pl.num_programs(0) - 1)
    def _():
        for e in range(N_EXPERTS):
            start = tail_ref[e]
            n = tail_ref[N_EXPERTS + e]

            def fill(r, carry):
                pltpu.make_async_copy(z_hbm.at[rows(0)], xs_hbm.at[rows(start + r)], sem).start()
                return carry

            def drain(r, carry):
                pltpu.make_async_copy(z_hbm.at[rows(0)], xs_hbm.at[rows(start + r)], sem).wait()
                return carry

            lax.fori_loop(0, n, fill, 0)
            lax.fori_loop(0, n, drain, 0)

        def fill_tile(j, carry):
            dst = xs_hbm.at[rows(j * tm, tm)]
            pltpu.make_async_copy(z_hbm, dst, sem).start()
            pltpu.make_async_copy(z_hbm, dst, sem).wait()
            return carry

        lax.fori_loop(tail_ref[2 * N_EXPERTS], nt, fill_tile, 0)


def _moe_dispatch(T, P, tm, pos, tail, u_slab):
    nblk = DISPATCH_BLK
    assert T % nblk == 0 and P % tm == 0
    kern = functools.partial(_moe_dispatch_kernel, nblk=nblk, tm=tm, nt=P // tm)
    zeros = jnp.zeros((tm * SLAB, LANE), F32)
    return pl.pallas_call(
        kern,
        grid=(T // nblk,),
        in_specs=[
            pl.BlockSpec((2 * nblk,), lambda i: (i,), memory_space=pltpu.SMEM),
            pl.BlockSpec(memory_space=pltpu.SMEM),
            pl.BlockSpec((nblk * SLAB, LANE), lambda i: (i, 0)),
            pl.BlockSpec(memory_space=pl.ANY),
        ],
        out_specs=pl.BlockSpec(memory_space=pl.ANY),
        out_shape=jax.ShapeDtypeStruct((P * SLAB, LANE), F32),
        scratch_shapes=[pltpu.SemaphoreType.DMA(())],
        compiler_params=_cparams(),
        name="moe_dispatch",
    )(pos, tail, u_slab, zeros)


def _moe_ffn_kernel(texp_ref, nv_ref, xs_ref, w1_ref, w3_ref, w2_ref, ys_ref):
    j = pl.program_id(0)
    tm = xs_ref.shape[0] // SLAB
    valid = j < nv_ref[0]

    @pl.when(valid)
    def _():
        u = _from_slab(xs_ref, tm).astype(BF16)
        _to_slab(ys_ref, _swiglu(u, w1_ref, w3_ref, w2_ref))

    @pl.when(jnp.logical_not(valid))
    def _():
        ys_ref[...] = jnp.zeros_like(ys_ref)


def _moe_ffn(tm, nt, texp, nv, xs2d, w1, w3, w2):
    wspec = lambda shp: pl.BlockSpec((None,) + shp, lambda j, te, nv_: (te[j], 0, 0))
    grid_spec = pltpu.PrefetchScalarGridSpec(
        num_scalar_prefetch=2,
        grid=(nt,),
        in_specs=[
            pl.BlockSpec((tm * SLAB, LANE), lambda j, te, nv_: (jnp.minimum(j, nv_[0] - 1), 0)),
            wspec(w1.shape[1:]), wspec(w3.shape[1:]), wspec(w2.shape[1:]),
        ],
        out_specs=pl.BlockSpec((tm * SLAB, LANE), lambda j, te, nv_: (j, 0)),
    )
    return pl.pallas_call(
        _moe_ffn_kernel,
        grid_spec=grid_spec,
        out_shape=jax.ShapeDtypeStruct(xs2d.shape, F32),
        compiler_params=_cparams(),
        name="moe_ffn",
    )(texp, nv, xs2d, w1, w3, w2)


def _moe_combine_kernel(pos_ref, x_ref, mod_ref, mf_ref, ys_hbm, o_ref, buf0, buf1, sem):
    tm = x_ref.shape[0]
    bufs = (buf0, buf1)

    def issue(r, carry):
        for k in range(2):
            dst = bufs[k].at[pl.ds(pl.multiple_of(r * SLAB, SLAB), SLAB)]
            pltpu.make_async_copy(ys_hbm.at[pos_ref[2 * r + k]], dst, sem).start(priority=k)
        return carry

    lax.fori_loop(0, tm, issue, 0, unroll=8)
    for k in range(2):
        pltpu.make_async_copy(bufs[1 - k], bufs[k], sem).wait()
    mf = mf_ref[...]
    g1 = mf[:, 0:1]
    g2 = mf[:, 1:2]
    for s in range(SLAB):
        cs = slice(s * LANE, (s + 1) * LANE)
        y = g1 * buf0[pl.ds(s, tm, stride=SLAB), :] + g2 * buf1[pl.ds(s, tm, stride=SLAB), :]
        o_ref[:, cs] = x_ref[:, cs] + mod_ref[5:6, cs] * y


def _moe_combine(tl, pos, x, mod, mf, ys):
    tm = tl.tm
    return pl.pallas_call(
        _moe_combine_kernel,
        grid=(tl.n,),
        in_specs=[
            pl.BlockSpec((2 * tm,), lambda i: (i,), memory_space=pltpu.SMEM),
            _row_spec(tl, D_MODEL), _mod_spec(tl), _row_spec(tl, LANE),
            pl.BlockSpec(memory_space=pl.ANY),
        ],
        out_specs=_row_spec(tl, D_MODEL),
        out_shape=jax.ShapeDtypeStruct((tl.T, D_MODEL), F32),
        scratch_shapes=[pltpu.VMEM((tm * SLAB, LANE), F32), pltpu.VMEM((tm * SLAB, LANE), F32),
                        pltpu.SemaphoreType.DMA(())],
        compiler_params=_cparams(),
        name="moe_combine",
    )(pos, x, mod, mf, ys)


def _moe_layer(tl, x, mod, g, w_router, w1, w3, w2):
    T, tm = tl.T, tl.tm
    nt = 2 * tl.n + N_EXPERTS
    P = nt * tm
    wr = jnp.pad(w_router, ((0, 0), (0, LANE - N_EXPERTS)))
    u_slab, mi, mf, cnt = _moe_route(tl, x, mod, g.reshape(1, -1), wr)
    counts = cnt[0, :N_EXPERTS].astype(I32)
    tiles_e = (counts + tm - 1) // tm
    cum_tiles = jnp.cumsum(tiles_e)
    nv = cum_tiles[-1:]
    off = (cum_tiles - tiles_e) * tm
    sel = mi[:, 0:2, None] == jnp.arange(N_EXPERTS, dtype=I32)
    pos = (jnp.sum(jnp.where(sel, off, 0), axis=-1) + mi[:, 2:4]).reshape(2 * T)
    jj = jnp.minimum(jnp.arange(nt, dtype=I32), nv[0] - 1)
    texp = jnp.sum((jj[:, None] >= cum_tiles[None, :]).astype(I32), axis=1)
    tail = jnp.concatenate([off + counts, tiles_e * tm - counts, nv])
    xs = _moe_dispatch(T, P, tm, pos, tail, u_slab)
    ys = _moe_ffn(tm, nt, texp, nv, xs,
                  w1.astype(BF16), w3.astype(BF16), w2.astype(BF16))
    return _moe_combine(tl, pos, x, mod, mf, ys.reshape(P, SLAB, LANE))


def _final_kernel(x_ref, g_ref, o_ref):
    o_ref[...] = _rms(x_ref[...], g_ref[...])


def _final_norm(tl, x, g, first_tile, n_tiles):
    return pl.pallas_call(
        _final_kernel,
        grid=(n_tiles,),
        in_specs=[pl.BlockSpec((tl.tm, D_MODEL), lambda i: (i + first_tile, 0)), _full((1, D_MODEL))],
        out_specs=_row_spec(tl, D_MODEL),
        out_shape=jax.ShapeDtypeStruct((n_tiles * tl.tm, D_MODEL), F32),
        compiler_params=_cparams(),
        name="final_norm",
    )(x, g.reshape(1, -1))


def kernel(x_prompt, x_sample, state_ssd, state_gla, state_lru, c, c_ctx, ada_w, ada_b, norm_g, final_g, ssd_w_in, ssd_conv_w, ssd_conv_b, ssd_a_log, ssd_dt_bias, ssd_d, ssd_norm_g, ssd_w_out, gla_w_in, gla_w_gate_up, gla_b_gate, gla_norm_g, gla_w_out, lru_w_in, lru_conv_w, lru_conv_b, lru_w_a, lru_b_a, lru_w_x, lru_b_x, lru_lambda, lru_w_out, ffn_w1, ffn_w3, ffn_w2, moe_router, moe_w1, moe_w3, moe_w2):
    n_ctx, len_ctx, _ = x_prompt.shape
    n_lat, len_lat, _ = x_sample.shape
    lay = Layout(n_ctx, len_ctx, n_lat, len_lat)
    tl = _Tiles(lay, TM)
    tl2 = tl.wide
    assert len_ctx & (len_ctx - 1) == 0 and len_ctx <= TM and TM % GRID_W == 0

    x = jnp.concatenate([x_prompt.reshape(-1, D_MODEL), x_sample.reshape(-1, D_MODEL)], axis=0)
    c_all = jnp.concatenate([c_ctx[None], c], axis=0)
    c_all = jnp.pad(c_all, ((0, -c_all.shape[0] % SUBLANE), (0, 0)))
    mods = _mods(c_all, ada_w, ada_b)

    caches = (state_ssd, state_gla, state_lru)
    ctx_states = ([], [], [])
    for i in range(DEPTH):
        kind, j = i % N_MIXERS, i // N_MIXERS
        cache = caches[kind][:, j]
        h0 = jnp.concatenate([jnp.zeros((n_ctx,) + cache.shape[1:], F32), cache], axis=0)
        if kind == 0:
            x, hfin = _ssd_layer(tl, lay, x, mods[i], norm_g[i, 0], h0, ssd_w_in[j], ssd_conv_w[j],
                                 ssd_conv_b[j], ssd_a_log[j], ssd_dt_bias[j], ssd_d[j],
                                 ssd_norm_g[j], ssd_w_out[j])
        elif kind == 1:
            x, hfin = _gla_layer(tl, lay, x, mods[i], norm_g[i, 0], h0, gla_w_in[j], gla_w_gate_up[j],
                                 gla_b_gate[j], gla_norm_g[j], gla_w_out[j])
        else:
            x, hfin = _lru_layer(tl, lay, x, mods[i], norm_g[i, 0], h0, lru_w_in[j], lru_conv_w[j],
                                 lru_conv_b[j], lru_w_a[j], lru_b_a[j], lru_w_x[j], lru_b_x[j],
                                 lru_lambda[j], lru_w_out[j])
        ctx_states[kind].append(hfin[:n_ctx])
        if i % 2 == 0:
            x = _ffn_layer(tl, x, mods[i], norm_g[i, 1], ffn_w1[i // 2], ffn_w3[i // 2], ffn_w2[i // 2])
        else:
            x = _moe_layer(tl, x, mods[i], norm_g[i, 1], moe_router[i // 2], moe_w1[i // 2],
                           moe_w3[i // 2], moe_w2[i // 2])

    y_ctx = _final_norm(tl2, x, final_g, 0, tl2.nct)
    y_lat = _final_norm(tl2, x, final_g, tl2.nct, tl2.n - tl2.nct)
    return (y_ctx.reshape(n_ctx, len_ctx, D_MODEL),
            y_lat.reshape(n_lat, len_lat, D_MODEL),
            jnp.stack(ctx_states[0], axis=1),
            jnp.stack(ctx_states[1], axis=1),
            jnp.stack(ctx_states[2], axis=1))
```

```python
import collections
import functools

import jax
import jax.numpy as jnp
import numpy as np
from jax import lax
from jax.experimental import pallas as pl
from jax.experimental.pallas import tpu as pltpu

F32 = jnp.float32
BF16 = jnp.bfloat16
I32 = jnp.int32

D_MODEL = 1024
DEPTH = 4
GRID_W = 64
N_MIXERS = 3
EPS = 1e-6
CONV_W = 4

SSD_INNER = 2 * D_MODEL
SSD_HEAD_DIM = 64
SSD_HEADS = SSD_INNER // SSD_HEAD_DIM
SSD_GROUPS = 8
SSD_GROUP_HEADS = SSD_HEADS // SSD_GROUPS
SSD_GROUP_W = SSD_GROUP_HEADS * SSD_HEAD_DIM
SSD_STATE = 128
SSD_CHUNK = 128
SSD_BC = SSD_GROUPS * SSD_STATE
SSD_CONV_CH = SSD_INNER + 2 * SSD_BC

GLA_HEADS = 4
GLA_KEY = D_MODEL // 2
GLA_VAL = D_MODEL
GLA_DK = GLA_KEY // GLA_HEADS
GLA_DV = GLA_VAL // GLA_HEADS
GLA_RANK = 16
GLA_TAU = 16.0
GLA_CHUNK = 64

LRU_WIDTH = D_MODEL
LRU_BLOCKS = 8
LRU_BLOCK = LRU_WIDTH // LRU_BLOCKS
LRU_C = 8.0

D_FF = 2816
N_EXPERTS = 8

LOG2E = 1.4426950408889634

LANE = 128
SUBLANE = 8
TM = 256
TM_WIDE = 512
SCAN_BLK = 256
VMEM_LIMIT = 56 * 1024 * 1024

Layout = collections.namedtuple("Layout", "n_ctx len_ctx n_lat len_lat")


def _cparams(n_axes=1):
    return pltpu.CompilerParams(
        dimension_semantics=("arbitrary",) * n_axes, vmem_limit_bytes=VMEM_LIMIT)


def _silu(x):
    return x * jax.nn.sigmoid(x)


def _softplus(x):
    return jnp.maximum(x, 0.0) + jnp.log(1.0 + jnp.exp(-jnp.abs(x)))


def _rms(x, g):
    return x * lax.rsqrt(jnp.mean(x * x, axis=-1, keepdims=True) + EPS) * g


def _dot(a, b):
    return jnp.dot(a, b, preferred_element_type=F32)


def _dot_nt(a, b):
    return lax.dot_general(a, b, (((1,), (1,)), ((), ())), preferred_element_type=F32)


def _dot_tn(a, b):
    return lax.dot_general(a, b, (((0,), (0,)), ((), ())), preferred_element_type=F32)


def _split3(x):
    hi = x.astype(BF16)
    r1 = x - hi.astype(F32)
    mid = r1.astype(BF16)
    lo = (r1 - mid.astype(F32)).astype(BF16)
    return hi, mid, lo


def _tri_cumsum(tri, x):
    hi, mid, lo = _split3(x)
    return _dot(tri, hi) + _dot(tri, mid) + _dot(tri, lo)


class _Tiles:
    def __init__(self, lay, tm):
        assert (lay.n_ctx * lay.len_ctx) % tm == 0 and lay.len_lat % tm == 0
        self.lay = lay
        self.tm = tm
        self.nct = lay.n_ctx * lay.len_ctx // tm
        self.tpl = lay.len_lat // tm
        self.n = self.nct + lay.n_lat * self.tpl
        self.T = self.n * tm

    def mod_row(self, i):
        return jnp.where(i < self.nct, 0, 1 + (i - self.nct) // self.tpl)

    @property
    def wide(self):
        return _Tiles(self.lay, TM_WIDE)


def _full(shape):
    nd = len(shape)
    return pl.BlockSpec(shape, lambda *_: (0,) * nd, pipeline_mode=pl.Buffered(1))


def _mod_spec(tl):
    return pl.BlockSpec((None, 6, D_MODEL), lambda i, *_: (tl.mod_row(i), 0, 0))


def _row_spec(tl, width):
    return pl.BlockSpec((tl.tm, width), lambda i, *_: (i, 0))


def _mods_kernel(c_ref, w_ref, b_ref, o_ref):
    c = c_ref[...]
    a = _silu(c)
    o_ref[...] = jnp.dot(a, w_ref[...], preferred_element_type=F32,
                         precision=lax.Precision.HIGHEST) + b_ref[...]


def _mods(c_all, ada_w, ada_b):
    R = c_all.shape[0]
    nb = 6
    out = pl.pallas_call(
        _mods_kernel,
        grid=(DEPTH, nb),
        in_specs=[
            pl.BlockSpec((R, D_MODEL), lambda l, j: (0, 0)),
            pl.BlockSpec((None, D_MODEL, D_MODEL), lambda l, j: (l, 0, j)),
            pl.BlockSpec((None, 1, D_MODEL), lambda l, j: (l, 0, j)),
        ],
        out_specs=pl.BlockSpec((None, R, D_MODEL), lambda l, j: (l, 0, j)),
        out_shape=jax.ShapeDtypeStruct((DEPTH, R, 6 * D_MODEL), F32),
        compiler_params=_cparams(2),
        name="mods",
    )(c_all, ada_w, ada_b.reshape(DEPTH, 1, 6 * D_MODEL))
    return out.reshape(DEPTH, R, 6, D_MODEL)


CONV_OFFSETS = (-1, 0, 1, 2)


def _shift_mats(tm, segs):
    t = np.arange(tm)
    out = []
    for seg in segs:
        mats = []
        for off in CONV_OFFSETS:
            if off == 0:
                continue
            src = t + off
            ok = (src >= 0) & (src < tm) & (src // seg == t // seg)
            m = np.zeros((tm, tm), np.float32)
            m[t[ok], src[ok]] = 1.0
            mats.append(m)
        out.append(np.concatenate(mats, axis=1))
    return jnp.asarray(np.stack(out), BF16)


def _shift_spec(tl):
    return pl.BlockSpec((None, tl.tm, 3 * tl.tm), lambda i: (jnp.where(i < tl.nct, 0, 1), 0, 0))


def _conv4(p, cw, cb, shifts):
    taps = [(p * cw[k:k + 1]).astype(BF16) for k in (0, 2, 3)]
    return cb + cw[1:2] * p + _dot(shifts, jnp.concatenate(taps, axis=0))


def _ssd_in_kernel(x_ref, mod_ref, g_ref, w_ref, wdt_ref, cw_ref, cb_ref, dtb_ref, sh_ref,
                   z_ref, xs_ref, b_ref, c_ref, dt_ref):
    mod = mod_ref[...]
    u = (_rms(x_ref[...], g_ref[...]) * (1.0 + mod[1:2]) + mod[0:1]).astype(BF16)
    z_ref[...] = _dot(u, w_ref[:, :SSD_INNER]).astype(z_ref.dtype)
    shifts = sh_ref[...]
    ch = 512
    for c0 in range(0, SSD_CONV_CH, ch):
        p = _dot(u, w_ref[:, SSD_INNER + c0:SSD_INNER + c0 + ch])
        y = _silu(_conv4(p, cw_ref[:, c0:c0 + ch], cb_ref[:, c0:c0 + ch], shifts))
        if c0 < SSD_INNER:
            dst, base, w = xs_ref, c0, SSD_GROUP_W
        elif c0 < SSD_INNER + SSD_BC:
            dst, base, w = b_ref, c0 - SSD_INNER, SSD_STATE
        else:
            dst, base, w = c_ref, c0 - SSD_INNER - SSD_BC, SSD_STATE
        for k in range(ch // w):
            dst[base // w + k] = y[:, k * w:(k + 1) * w].astype(dst.dtype)
    dt_ref[...] = _softplus(_dot(u, wdt_ref[...]) + dtb_ref[...])


def _ssd_in(tl, lay, x, mod, g, w_main, w_dt, conv_w, conv_b, dt_bias):
    T = tl.T
    gspec = lambda w: pl.BlockSpec((SSD_GROUPS, tl.tm, w), lambda i: (0, i, 0))
    return pl.pallas_call(
        _ssd_in_kernel,
        grid=(tl.n,),
        in_specs=[
            _row_spec(tl, D_MODEL), _mod_spec(tl), _full((1, D_MODEL)),
            _full(w_main.shape), _full(w_dt.shape), _full(conv_w.shape),
            _full(conv_b.shape), _full(dt_bias.shape), _shift_spec(tl),
        ],
        out_specs=[
            _row_spec(tl, SSD_INNER), gspec(SSD_GROUP_W), gspec(SSD_STATE), gspec(SSD_STATE),
            _row_spec(tl, 2 * LANE),
        ],
        out_shape=[
            jax.ShapeDtypeStruct((T, SSD_INNER), BF16),
            jax.ShapeDtypeStruct((SSD_GROUPS, T, SSD_GROUP_W), BF16),
            jax.ShapeDtypeStruct((SSD_GROUPS, T, SSD_STATE), BF16),
            jax.ShapeDtypeStruct((SSD_GROUPS, T, SSD_STATE), BF16),
            jax.ShapeDtypeStruct((T, 2 * LANE), F32),
        ],
        compiler_params=_cparams(),
        name="ssd_in",
    )(x, mod, g, w_main, w_dt, conv_w, conv_b, dt_bias, _shift_mats(tl.tm, (lay.len_ctx, GRID_W)))


def _ssd_scan_kernel(bc_ref, seq_ref, first_ref, last_ref,
                     xf_ref, bf_ref, cf_ref, dtf_ref, xb_ref, bb_ref, cb_ref, dtb_ref,
                     alog_ref, h0_ref, yf_ref, yb_ref, hout_ref, h_s, cum_s, crow_s, wrow_s):
    i = pl.program_id(0)
    Q = SSD_CHUNK
    P = SSD_HEAD_DIM
    pairs_per_group = SSD_GROUP_HEADS // 2

    @pl.when(first_ref[i] == 1)
    def _():
        h_s[...] = h0_ref[...]

    rows = lax.broadcasted_iota(I32, (Q, Q), 0)
    cols = lax.broadcasted_iota(I32, (Q, Q), 1)
    masks = (rows >= cols, rows <= cols)
    first_head = lax.broadcasted_iota(I32, (1, LANE), 1) < P
    dirs = ((xf_ref, bf_ref, cf_ref, dtf_ref, yf_ref), (xb_ref, bb_ref, cb_ref, dtb_ref, yb_ref))

    for d in range(2):
        x_ref, b_ref, c_ref, dt_ref, y_ref = dirs[d]
        end = Q - 1 if d == 0 else 0
        dt = dt_ref[...]
        cum = _tri_cumsum(masks[d].astype(BF16), -dt * jnp.exp(alog_ref[d:d + 1]))
        cum_t = cum.T
        dt_t = dt.T
        cum_s[d] = cum * LOG2E
        crow_s[d] = (cum_t - jnp.log(dt_t)) * LOG2E
        wrow_s[d] = dt_t * jnp.exp(cum_t[:, end:end + 1] - cum_t)
        for g in range(SSD_GROUPS):
            bg = b_ref[g]
            cg = c_ref[g]
            cb = _dot_nt(cg, bg)
            cgf = cg.astype(F32)
            bgt = bg.astype(F32).T
            for j in range(pairs_per_group):
                p = g * pairs_per_group + j
                xp = x_ref[g, :, j * LANE:(j + 1) * LANE]
                zero = jnp.zeros_like(xp)
                xa = jnp.where(first_head, xp, zero)
                xb = jnp.where(first_head, zero, xp)
                hp = h_s[d, p]
                hpb = hp.astype(BF16)
                ha = jnp.where(first_head, hpb, jnp.zeros_like(hpb))
                hb = jnp.where(first_head, jnp.zeros_like(hpb), hpb)
                lhs, bws, tots = [], [], []
                for h in (2 * p, 2 * p + 1):
                    ccol = jnp.broadcast_to(cum_s[d, :, h:h + 1], (Q, Q))
                    ecol = jnp.exp2(ccol)
                    dec = jnp.where(masks[d], jnp.exp2(ccol - crow_s[d, h:h + 1, :]), 0.0)
                    lhs += [(cb * dec).astype(BF16), (cgf * ecol).astype(BF16)]
                    bws.append((bgt * wrow_s[d, h:h + 1, :]).astype(BF16))
                    tots.append(ecol[end:end + 1])
                y = _dot(jnp.concatenate(lhs, axis=1), jnp.concatenate([xa, ha, xb, hb], axis=0))
                y_ref[g, :, j * LANE:(j + 1) * LANE] = y.astype(y_ref.dtype)
                upd = _dot(jnp.concatenate(bws, axis=1), jnp.concatenate([xa, xb], axis=0))
                h_s[d, p] = hp * jnp.where(first_head, tots[0], tots[1]) + upd

    @pl.when(last_ref[i] == 1)
    def _():
        hout_ref[...] = h_s[...]


def _scan_tables(lay, blk):
    bc, seq, first, last = [], [], [], []
    s0 = 0
    sid = 0
    for n, L in ((lay.n_ctx, lay.len_ctx), (lay.n_lat, lay.len_lat)):
        nb = L // blk
        for _ in range(n):
            for k in range(nb):
                bc.append(s0 + nb - 1 - k)
                seq.append(sid)
                first.append(int(k == 0))
                last.append(int(k == nb - 1))
            s0 += nb
            sid += 1
    return tuple(jnp.asarray(np.asarray(a, np.int32)) for a in (bc, seq, first, last))


def _ssd_scan(lay, xs, bm, cm, dt, a_log, h0):
    Q = SSD_CHUNK
    T = xs.shape[1]
    tabs = _scan_tables(lay, Q)
    nseq = lay.n_ctx + lay.n_lat
    G, N, W = SSD_GROUPS, SSD_STATE, SSD_GROUP_W
    st_shape = (2, SSD_HEADS // 2, N, 2 * SSD_HEAD_DIM)
    assert 2 * SSD_HEAD_DIM == LANE and Q == LANE
    fwd = lambda w: pl.BlockSpec((G, Q, w), lambda i, bc, sq, fi, la: (0, i, 0))
    bwd = lambda w: pl.BlockSpec((G, Q, w), lambda i, bc, sq, fi, la: (0, bc[i], 0))
    st_spec = pl.BlockSpec((None,) + st_shape, lambda i, bc, sq, fi, la: (sq[i], 0, 0, 0, 0))
    grid_spec = pltpu.PrefetchScalarGridSpec(
        num_scalar_prefetch=4,
        grid=(T // Q,),
        in_specs=[
            fwd(W), fwd(N), fwd(N), pl.BlockSpec((Q, LANE), lambda i, bc, sq, fi, la: (i, 0)),
            bwd(W), bwd(N), bwd(N), pl.BlockSpec((Q, LANE), lambda i, bc, sq, fi, la: (bc[i], 1)),
            pl.BlockSpec((2, LANE), lambda i, *_: (0, 0)),
            st_spec,
        ],
        out_specs=[fwd(W), bwd(W), st_spec],
        scratch_shapes=[pltpu.VMEM(st_shape, F32), pltpu.VMEM((2, Q, LANE), F32),
                        pltpu.VMEM((2, LANE, Q), F32), pltpu.VMEM((2, LANE, Q), F32)],
    )
    return pl.pallas_call(
        _ssd_scan_kernel,
        grid_spec=grid_spec,
        out_shape=[
            jax.ShapeDtypeStruct((G, T, W), BF16),
            jax.ShapeDtypeStruct((G, T, W), BF16),
            jax.ShapeDtypeStruct((nseq,) + st_shape, F32),
        ],
        compiler_params=_cparams(),
        name="ssd_scan",
    )(*tabs, xs, bm, cm, dt, xs, bm, cm, dt, a_log, h0)


def _ffn_tail(x, mod, ffn_refs):
    if not ffn_refs:
        return x
    g_ref, w1_ref, w3_ref, w2_ref = ffn_refs
    u = (_rms(x, g_ref[...]) * (1.0 + mod[4:5]) + mod[3:4]).astype(BF16)
    return x + mod[5:6] * _swiglu(u, w1_ref, w3_ref, w2_ref)


def _ffn_operands(ffn):
    if ffn is None:
        return [], []
    g, w1, w3, w2 = ffn
    ops = [g.reshape(1, -1), w1.astype(BF16), w3.astype(BF16), w2.astype(BF16)]
    return ops, [_full(o.shape) for o in ops]


def _ssd_out_kernel(yf_ref, yb_ref, xs_ref, z_ref, x_ref, mod_ref, d_ref, ng_ref, w_ref, *rest):
    *ffn_refs, o_ref = rest
    mod = mod_ref[...]
    parts = []
    for g in range(SSD_GROUPS):
        dg = d_ref[:, g * SSD_GROUP_W:(g + 1) * SSD_GROUP_W]
        parts.append(yf_ref[g].astype(F32) + yb_ref[g].astype(F32) + dg * xs_ref[g].astype(F32))
    y = jnp.concatenate(parts, axis=1)
    y = _rms(y * _silu(z_ref[...].astype(F32)), ng_ref[...]).astype(BF16)
    o_ref[...] = _ffn_tail(x_ref[...] + mod[2:3] * _dot(y, w_ref[...]), mod, ffn_refs)


def _ssd_out(tl, yf, yb, xs, z, x, mod, d_exp, norm_g, w_out, ffn=None):
    tl = tl if ffn is not None else tl.wide
    ffn_ops, ffn_specs = _ffn_operands(ffn)
    gspec = pl.BlockSpec((SSD_GROUPS, tl.tm, SSD_GROUP_W), lambda i: (0, i, 0))
    return pl.pallas_call(
        _ssd_out_kernel,
        grid=(tl.n,),
        in_specs=[gspec, gspec, gspec, _row_spec(tl, SSD_INNER), _row_spec(tl, D_MODEL),
                  _mod_spec(tl), _full(d_exp.shape), _full(norm_g.shape), _full(w_out.shape)] + ffn_specs,
        out_specs=_row_spec(tl, D_MODEL),
        out_shape=jax.ShapeDtypeStruct((tl.T, D_MODEL), F32),
        compiler_params=_cparams(),
        name="ssd_out",
    )(yf, yb, xs, z, x, mod, d_exp, norm_g, w_out, *ffn_ops)


def _ssd_layer(tl, lay, x, mod, g, h0, w_in, conv_w, conv_b, a_log, dt_bias, d_skip, norm_g, w_out,
               ffn=None):
    nseq = h0.shape[0]
    split = SSD_INNER + SSD_CONV_CH
    w_main = w_in[:, :split].astype(BF16)
    lane_pad = lambda v: jnp.pad(v, ((0, 0), (0, 0), (0, LANE - SSD_HEADS)))
    w_dt = lane_pad(w_in[:, split:].reshape(D_MODEL, 2, SSD_HEADS)).reshape(D_MODEL, 2 * LANE).astype(BF16)
    dtb = lane_pad(dt_bias.reshape(1, 2, SSD_HEADS)).reshape(1, 2 * LANE)
    alog = lane_pad(a_log.reshape(1, 2, SSD_HEADS)).reshape(2, LANE)
    z, xs, bm, cm, dt = _ssd_in(tl, lay, x, mod, g.reshape(1, -1), w_main, w_dt,
                                conv_w, conv_b.reshape(1, -1), dtb)
    pair_shape = (nseq, 2, SSD_HEADS // 2, 2 * SSD_HEAD_DIM, SSD_STATE)
    yf, yb, hfin = _ssd_scan(lay, xs, bm, cm, dt, alog, h0.reshape(pair_shape).swapaxes(-1, -2))
    hfin = hfin.swapaxes(-1, -2).reshape(h0.shape)
    d_exp = jnp.repeat(d_skip, SSD_HEAD_DIM).reshape(1, -1)
    x = _ssd_out(tl, yf, yb, xs, z, x, mod, d_exp, norm_g.reshape(1, -1), w_out.astype(BF16), ffn)
    return x, hfin


def _gla_in_kernel(x_ref, mod_ref, g_ref, w_ref, wgd_ref, wup_ref, bg_ref,
                   q_ref, k_ref, v_ref, r_ref, la_ref):
    mod = mod_ref[...]
    u = (_rms(x_ref[...], g_ref[...]) * (1.0 + mod[1:2]) + mod[0:1]).astype(BF16)
    q_ref[...] = (_dot(u, w_ref[:, :GLA_KEY]) * (GLA_DK ** -0.5)).astype(q_ref.dtype)
    k_ref[...] = _dot(u, w_ref[:, GLA_KEY:2 * GLA_KEY]).astype(k_ref.dtype)
    v_ref[...] = _dot(u, w_ref[:, 2 * GLA_KEY:2 * GLA_KEY + GLA_VAL]).astype(v_ref.dtype)
    r_ref[...] = _dot(u, w_ref[:, 2 * GLA_KEY + GLA_VAL:]).astype(r_ref.dtype)
    gd = _dot(u, wgd_ref[...]).astype(BF16)
    logit = _dot(gd, wup_ref[...]) + bg_ref[...]
    la_ref[...] = (jnp.minimum(logit, 0.0) - jnp.log(1.0 + jnp.exp(-jnp.abs(logit)))) / GLA_TAU


def _gla_in(tl, x, mod, g, w_main, w_gd, w_up, b_gate):
    tl = tl.wide
    T = tl.T
    return pl.pallas_call(
        _gla_in_kernel,
        grid=(tl.n,),
        in_specs=[_row_spec(tl, D_MODEL), _mod_spec(tl), _full((1, D_MODEL)), _full(w_main.shape),
                  _full(w_gd.shape), _full(w_up.shape), _full(b_gate.shape)],
        out_specs=[_row_spec(tl, GLA_KEY), _row_spec(tl, GLA_KEY), _row_spec(tl, GLA_VAL),
                   _row_spec(tl, GLA_VAL), _row_spec(tl, 2 * GLA_KEY)],
        out_shape=[
            jax.ShapeDtypeStruct((T, GLA_KEY), BF16),
            jax.ShapeDtypeStruct((T, GLA_KEY), BF16),
            jax.ShapeDtypeStruct((T, GLA_VAL), BF16),
            jax.ShapeDtypeStruct((T, GLA_VAL), BF16),
            jax.ShapeDtypeStruct((T, 2 * GLA_KEY), F32),
        ],
        compiler_params=_cparams(),
        name="gla_in",
    )(x, mod, g, w_main, w_gd, w_up, b_gate)


def _gla_scan_kernel(bc_ref, seq_ref, first_ref, last_ref,
                     qf_ref, kf_ref, vf_ref, laf_ref, qb_ref, kb_ref, vb_ref, lab_ref, s0_ref,
                     of_ref, ob_ref, sout_ref, s_s):
    i = pl.program_id(0)
    C = GLA_CHUNK
    nsub = qf_ref.shape[0] // C

    @pl.when(first_ref[i] == 1)
    def _():
        s_s[...] = s0_ref[...]

    nrow = qf_ref.shape[0]
    rows = lax.broadcasted_iota(I32, (nrow, nrow), 0)
    cols = lax.broadcasted_iota(I32, (nrow, nrow), 1)
    shift = C.bit_length() - 1
    assert C == 1 << shift
    same_chunk = jnp.right_shift(rows, shift) == jnp.right_shift(cols, shift)
    tris = (jnp.where(same_chunk & (rows >= cols), 1.0, 0.0).astype(BF16),
            jnp.where(same_chunk & (rows <= cols), 1.0, 0.0).astype(BF16))
    rows_c = lax.broadcasted_iota(I32, (C, C), 0)
    cols_c = lax.broadcasted_iota(I32, (C, C), 1)
    masks = (rows_c >= cols_c, rows_c <= cols_c)
    dirs = ((qf_ref, kf_ref, vf_ref, laf_ref, of_ref), (qb_ref, kb_ref, vb_ref, lab_ref, ob_ref))

    for d in range(2):
        q_ref, k_ref, v_ref, la_ref, o_ref = dirs[d]
        gcum_all = _tri_cumsum(tris[d], la_ref[:, d * GLA_KEY:(d + 1) * GLA_KEY])
        for c in range(nsub):
            cc = c if d == 0 else nsub - 1 - c
            rs = slice(cc * C, (cc + 1) * C)
            for h in range(GLA_HEADS):
                ks = slice(h * GLA_DK, (h + 1) * GLA_DK)
                vs = slice(h * GLA_DV, (h + 1) * GLA_DV)
                gcum = gcum_all[rs, ks]
                qh = q_ref[rs, ks].astype(F32)
                kh = k_ref[rs, ks].astype(F32)
                vh = v_ref[rs, vs]
                qi = (qh * jnp.exp(gcum)).astype(BF16)
                ki = (kh * jnp.exp(-gcum)).astype(BF16)
                att = jnp.where(masks[d], _dot_nt(qi, ki), 0.0).astype(BF16)
                st = s_s[d, h]
                o_ref[rs, vs] = (_dot(att, vh) + _dot_nt(qi, st.astype(BF16))).astype(o_ref.dtype)
                glast = gcum[C - 1:C] if d == 0 else gcum[0:1]
                kdec = (kh * jnp.exp(glast - gcum)).astype(BF16)
                s_s[d, h] = st * jnp.exp(glast) + _dot_tn(vh, kdec)

    @pl.when(last_ref[i] == 1)
    def _():
        sout_ref[...] = s_s[...]


def _gla_scan(lay, q, k, v, la, s0):
    B = SCAN_BLK
    T = q.shape[0]
    tabs = _scan_tables(lay, B)
    nseq = lay.n_ctx + lay.n_lat
    fwd = lambda w: pl.BlockSpec((B, w), lambda i, bc, sq, fi, la_: (i, 0))
    bwd = lambda w: pl.BlockSpec((B, w), lambda i, bc, sq, fi, la_: (bc[i], 0))
    st_spec = pl.BlockSpec((None, 2, GLA_HEADS, GLA_DV, GLA_DK),
                           lambda i, bc, sq, fi, la_: (sq[i], 0, 0, 0, 0))
    grid_spec = pltpu.PrefetchScalarGridSpec(
        num_scalar_prefetch=4,
        grid=(T // B,),
        in_specs=[fwd(GLA_KEY), fwd(GLA_KEY), fwd(GLA_VAL), fwd(2 * GLA_KEY),
                  bwd(GLA_KEY), bwd(GLA_KEY), bwd(GLA_VAL), bwd(2 * GLA_KEY), st_spec],
        out_specs=[fwd(GLA_VAL), bwd(GLA_VAL), st_spec],
        scratch_shapes=[pltpu.VMEM((2, GLA_HEADS, GLA_DV, GLA_DK), F32)],
    )
    return pl.pallas_call(
        _gla_scan_kernel,
        grid_spec=grid_spec,
        out_shape=[
            jax.ShapeDtypeStruct((T, GLA_VAL), BF16),
            jax.ShapeDtypeStruct((T, GLA_VAL), BF16),
            jax.ShapeDtypeStruct((nseq, 2, GLA_HEADS, GLA_DV, GLA_DK), F32),
        ],
        compiler_params=_cparams(),
        name="gla_scan",
    )(*tabs, q, k, v, la, q, k, v, la, s0)


def _gla_out_kernel(of_ref, ob_ref, r_ref, x_ref, mod_ref, ng_ref, w_ref, o_ref):
    o = of_ref[...].astype(F32) + ob_ref[...].astype(F32)
    ng = ng_ref[...]
    parts = [_rms(o[:, h * GLA_DV:(h + 1) * GLA_DV], ng) for h in range(GLA_HEADS)]
    y = (jnp.concatenate(parts, axis=1) * _silu(r_ref[...].astype(F32))).astype(BF16)
    o_ref[...] = x_ref[...] + mod_ref[2:3] * _dot(y, w_ref[...])


def _gla_out(tl, of, ob, r, x, mod, norm_g, w_out):
    tl = tl.wide
    return pl.pallas_call(
        _gla_out_kernel,
        grid=(tl.n,),
        in_specs=[_row_spec(tl, GLA_VAL), _row_spec(tl, GLA_VAL), _row_spec(tl, GLA_VAL),
                  _row_spec(tl, D_MODEL), _mod_spec(tl), _full(norm_g.shape), _full(w_out.shape)],
        out_specs=_row_spec(tl, D_MODEL),
        out_shape=jax.ShapeDtypeStruct((tl.T, D_MODEL), F32),
        compiler_params=_cparams(),
        name="gla_out",
    )(of, ob, r, x, mod, norm_g, w_out)


def _gla_layer(tl, lay, x, mod, g, s0, w_in, w_gate_up, b_gate, norm_g, w_out):
    split = 2 * GLA_KEY + 2 * GLA_VAL
    w_main = w_in[:, :split].astype(BF16)
    w_gd = jnp.pad(w_in[:, split:], ((0, 0), (0, LANE - 2 * GLA_RANK))).astype(BF16)
    w_up = jnp.zeros((LANE, 2 * GLA_KEY), F32)
    for d in range(2):
        w_up = w_up.at[d * GLA_RANK:(d + 1) * GLA_RANK, d * GLA_KEY:(d + 1) * GLA_KEY].set(w_gate_up[d])
    q, k, v, r, la = _gla_in(tl, x, mod, g.reshape(1, -1), w_main, w_gd, w_up.astype(BF16),
                             b_gate.reshape(1, -1))
    of, ob, sfin = _gla_scan(lay, q, k, v, la, s0.swapaxes(-1, -2))
    x = _gla_out(tl, of, ob, r, x, mod, norm_g.reshape(1, -1), w_out.astype(BF16))
    return x, sfin.swapaxes(-1, -2)


def _gelu_tanh(x):
    return 0.5 * x * (1.0 + jnp.tanh(0.7978845608028654 * (x + 0.044715 * (x * x * x))))


def _lru_in_kernel(*refs, pending):
    if pending:
        (pos_ref, posn_ref, modp_ref, mf_ref, ys_hbm, x_ref, mod_ref, g_ref, w_ref, cw_ref, cb_ref,
         wg_ref, bg_ref, lam_ref, sh_ref, a_ref, b_ref, gg_ref, xo_ref, buf, sem) = refs
        i = pl.program_id(0)
        tm = x_ref.shape[0]

        def gather(p_ref, slot):
            def issue(r, carry):
                for k in range(2):
                    dst = buf.at[slot, k, pl.ds(pl.multiple_of(r * SLAB, SLAB), SLAB)]
                    pltpu.make_async_copy(ys_hbm.at[p_ref[2 * r + k]], dst, sem.at[slot]).start(priority=k)
                return carry
            lax.fori_loop(0, tm, issue, 0, unroll=8)

        @pl.when(i == 0)
        def _():
            gather(pos_ref, 0)

        for slot in range(2):
            @pl.when((i % 2 == slot) & (i + 1 < pl.num_programs(0)))
            def _():
                gather(posn_ref, 1 - slot)

            @pl.when(i % 2 == slot)
            def _():
                for k in range(2):
                    pltpu.make_async_copy(buf.at[slot, 1 - k], buf.at[slot, k], sem.at[slot]).wait()
                mf = mf_ref[...]
                for s in range(SLAB):
                    cs = slice(s * LANE, (s + 1) * LANE)
                    y = (mf[:, 0:1] * buf[slot, 0, pl.ds(s, tm, stride=SLAB), :]
                         + mf[:, 1:2] * buf[slot, 1, pl.ds(s, tm, stride=SLAB), :])
                    xo_ref[:, cs] = x_ref[:, cs] + modp_ref[5:6, cs] * y

        x = xo_ref[...]
    else:
        (x_ref, mod_ref, g_ref, w_ref, cw_ref, cb_ref, wg_ref, bg_ref, lam_ref, sh_ref,
         a_ref, b_ref, gg_ref) = refs
        x = x_ref[...]
    W = LRU_WIDTH
    mod = mod_ref[...]
    u = (_rms(x, g_ref[...]) * (1.0 + mod[1:2]) + mod[0:1]).astype(BF16)
    gg_ref[...] = _gelu_tanh(_dot(u, w_ref[:, :W])).astype(gg_ref.dtype)
    xb = _conv4(_dot(u, w_ref[:, W:]), cw_ref[...], cb_ref[...], sh_ref[...])
    sp = _softplus(-lam_ref[...])
    for n in range(LRU_BLOCKS):
        cs = slice(n * LRU_BLOCK, (n + 1) * LRU_BLOCK)
        xn = xb[:, cs]
        gates = jax.nn.sigmoid(_dot(xn.astype(BF16), wg_ref[n]) + bg_ref[n])
        for d in range(2):
            r = gates[:, d * LRU_BLOCK:(d + 1) * LRU_BLOCK]
            ig = gates[:, (2 + d) * LRU_BLOCK:(3 + d) * LRU_BLOCK]
            log_a = (-LRU_C) * r * sp[:, d * W + n * LRU_BLOCK:d * W + (n + 1) * LRU_BLOCK]
            a = jnp.exp(log_a)
            ds = slice(d * W + n * LRU_BLOCK, d * W + (n + 1) * LRU_BLOCK)
            a_ref[:, ds] = a
            b_ref[:, ds] = jnp.sqrt(1.0 - a * a) * (ig * xn)


def _lru_in(tl, lay, x, mod, g, w_in, conv_w, conv_b, w_gates, b_gates, lam, pending=None):
    T, tm = tl.T, tl.tm
    operands = [x, mod, g, w_in, conv_w, conv_b, w_gates, b_gates, lam,
                _shift_mats(tm, (lay.len_ctx, GRID_W))]
    in_specs = [_row_spec(tl, D_MODEL), _mod_spec(tl), _full((1, D_MODEL)), _full(w_in.shape),
                _full(conv_w.shape), _full(conv_b.shape), _full(w_gates.shape),
                _full(b_gates.shape), _full(lam.shape), _shift_spec(tl)]
    out_specs = [_row_spec(tl, 2 * LRU_WIDTH), _row_spec(tl, 2 * LRU_WIDTH), _row_spec(tl, LRU_WIDTH)]
    out_shape = [jax.ShapeDtypeStruct((T, 2 * LRU_WIDTH), F32),
                 jax.ShapeDtypeStruct((T, 2 * LRU_WIDTH), F32),
                 jax.ShapeDtypeStruct((T, LRU_WIDTH), BF16)]
    scratch = []
    if pending is not None:
        pos, mod_prev, mf, ys = pending
        last = tl.n - 1
        operands = [pos, pos, mod_prev, mf, ys] + operands
        in_specs = [pl.BlockSpec((2 * tm,), lambda i: (i,), memory_space=pltpu.SMEM),
                    pl.BlockSpec((2 * tm,), lambda i: (jnp.minimum(i + 1, last),), memory_space=pltpu.SMEM),
                    _mod_spec(tl), _row_spec(tl, LANE), pl.BlockSpec(memory_space=pl.ANY)] + in_specs
        out_specs.append(_row_spec(tl, D_MODEL))
        out_shape.append(jax.ShapeDtypeStruct((T, D_MODEL), F32))
        scratch = [pltpu.VMEM((2, 2, tm * SLAB, LANE), F32), pltpu.SemaphoreType.DMA((2,))]
    return pl.pallas_call(
        functools.partial(_lru_in_kernel, pending=pending is not None),
        grid=(tl.n,),
        in_specs=in_specs,
        out_specs=out_specs,
        out_shape=out_shape,
        scratch_shapes=scratch,
        compiler_params=_cparams(),
        name="lru_in",
    )(*operands)


def _lru_scan_kernel(bc_ref, seq_ref, first_ref, last_ref,
                     af_ref, bf_ref, ab_ref, bb_ref, h0_ref, hf_ref, hb_ref, hout_ref, h_s):
    i = pl.program_id(0)
    S = SUBLANE
    W = LRU_WIDTH
    ntile = af_ref.shape[0] // S

    @pl.when(first_ref[i] == 1)
    def _():
        h_s[...] = h0_ref[...]

    ri = lax.broadcasted_iota(I32, (S, W), 0)

    def tile(j, carry):
        hf, hb = carry
        rs = pl.ds(pl.multiple_of(j * S, S), S)
        a = af_ref[rs, 0:W]
        b = bf_ref[rs, 0:W]
        for sh in (1, 2, 4):
            keep = ri >= sh
            a_s = jnp.where(keep, pltpu.roll(a, sh, 0), 1.0)
            b_s = jnp.where(keep, pltpu.roll(b, sh, 0), 0.0)
            b = a * b_s + b
            a = a * a_s
        h8 = a * hf + b
        hf_ref[rs, :] = h8
        hf = h8[S - 1:S]
        rs = pl.ds(pl.multiple_of((ntile - 1 - j) * S, S), S)
        a = ab_ref[rs, W:2 * W]
        b = bb_ref[rs, W:2 * W]
        for sh in (1, 2, 4):
            keep = ri < S - sh
            a_s = jnp.where(keep, pltpu.roll(a, S - sh, 0), 1.0)
            b_s = jnp.where(keep, pltpu.roll(b, S - sh, 0), 0.0)
            b = a * b_s + b
            a = a * a_s
        h8 = a * hb + b
        hb_ref[rs, :] = h8
        hb = h8[0:1]
        return hf, hb

    hf, hb = lax.fori_loop(0, ntile, tile, (h_s[0:1], h_s[1:2]))
    h_s[0:1] = hf
    h_s[1:2] = hb

    @pl.when(last_ref[i] == 1)
    def _():
        hout_ref[...] = h_s[...]


def _lru_scan(lay, a, b, h0):
    B = SCAN_BLK
    T = a.shape[0]
    W = LRU_WIDTH
    tabs = _scan_tables(lay, B)
    nseq = lay.n_ctx + lay.n_lat
    fwd = lambda w: pl.BlockSpec((B, w), lambda i, bc, sq, fi, la_: (i, 0))
    bwd = lambda w: pl.BlockSpec((B, w), lambda i, bc, sq, fi, la_: (bc[i], 0))
    st_spec = pl.BlockSpec((None, 2, W), lambda i, bc, sq, fi, la_: (sq[i], 0, 0))
    grid_spec = pltpu.PrefetchScalarGridSpec(
        num_scalar_prefetch=4,
        grid=(T // B,),
        in_specs=[fwd(2 * W), fwd(2 * W), bwd(2 * W), bwd(2 * W), st_spec],
        out_specs=[fwd(W), bwd(W), st_spec],
        scratch_shapes=[pltpu.VMEM((2, W), F32)],
    )
    return pl.pallas_call(
        _lru_scan_kernel,
        grid_spec=grid_spec,
        out_shape=[
            jax.ShapeDtypeStruct((T, W), F32),
            jax.ShapeDtypeStruct((T, W), F32),
            jax.ShapeDtypeStruct((nseq, 2, W), F32),
        ],
        compiler_params=_cparams(),
        name="lru_scan",
    )(*tabs, a, b, a, b, h0)


def _lru_out_kernel(hf_ref, hb_ref, gg_ref, x_ref, mod_ref, w_ref, *rest):
    *ffn_refs, o_ref = rest
    mod = mod_ref[...]
    y = ((hf_ref[...] + hb_ref[...]) * gg_ref[...].astype(F32)).astype(BF16)
    o_ref[...] = _ffn_tail(x_ref[...] + mod[2:3] * _dot(y, w_ref[...]), mod, ffn_refs)


def _lru_out(tl, hf, hb, gg, x, mod, w_out, ffn=None):
    tl = tl if ffn is not None else tl.wide
    ffn_ops, ffn_specs = _ffn_operands(ffn)
    W = LRU_WIDTH
    return pl.pallas_call(
        _lru_out_kernel,
        grid=(tl.n,),
        in_specs=[_row_spec(tl, W), _row_spec(tl, W), _row_spec(tl, W), _row_spec(tl, D_MODEL),
                  _mod_spec(tl), _full(w_out.shape)] + ffn_specs,
        out_specs=_row_spec(tl, D_MODEL),
        out_shape=jax.ShapeDtypeStruct((tl.T, D_MODEL), F32),
        compiler_params=_cparams(),
        name="lru_out",
    )(hf, hb, gg, x, mod, w_out, *ffn_ops)


def _lru_layer(tl, lay, x, mod, g, h0, w_in, conv_w, conv_b, w_a, b_a, w_x, b_x, lam, w_out,
               ffn=None, pending=None):
    w_gates = jnp.concatenate([w_a[0], w_a[1], w_x[0], w_x[1]], axis=-1).astype(BF16)
    blk = lambda v: v.reshape(2, LRU_BLOCKS, LRU_BLOCK)
    b_gates = jnp.concatenate([blk(b_a)[0], blk(b_a)[1], blk(b_x)[0], blk(b_x)[1]], axis=-1)
    b_gates = b_gates.reshape(LRU_BLOCKS, 1, 4 * LRU_BLOCK)
    outs = _lru_in(tl, lay, x, mod, g.reshape(1, -1), w_in.astype(BF16), conv_w,
                   conv_b.reshape(1, -1), w_gates, b_gates, lam.reshape(1, -1), pending)
    a, b, gg = outs[:3]
    if pending is not None:
        x = outs[3]
    hf, hb, hfin = _lru_scan(lay, a, b, h0)
    x = _lru_out(tl, hf, hb, gg, x, mod, w_out.astype(BF16), ffn)
    return x, hfin


def _swiglu(u, w1_ref, w3_ref, w2_ref):
    a = _dot(u, w1_ref[...])
    b = _dot(u, w3_ref[...])
    return _dot((_silu(a) * b).astype(BF16), w2_ref[...])


def _ffn_kernel(x_ref, mod_ref, g_ref, w1_ref, w3_ref, w2_ref, o_ref):
    x = x_ref[...]
    mod = mod_ref[...]
    u = (_rms(x, g_ref[...]) * (1.0 + mod[4:5]) + mod[3:4]).astype(BF16)
    o_ref[...] = x + mod[5:6] * _swiglu(u, w1_ref, w3_ref, w2_ref)


def _ffn_layer(tl, x, mod, g, w1, w3, w2):
    tl = tl.wide
    w1, w3, w2 = w1.astype(BF16), w3.astype(BF16), w2.astype(BF16)
    return pl.pallas_call(
        _ffn_kernel,
        grid=(tl.n,),
        in_specs=[_row_spec(tl, D_MODEL), _mod_spec(tl), _full((1, D_MODEL)),
                  _full(w1.shape), _full(w3.shape), _full(w2.shape)],
        out_specs=_row_spec(tl, D_MODEL),
        out_shape=jax.ShapeDtypeStruct((tl.T, D_MODEL), F32),
        compiler_params=_cparams(),
        name="ffn",
    )(x, mod, g.reshape(1, -1), w1, w3, w2)


SLAB = D_MODEL // LANE
assert SLAB == SUBLANE


def _to_slab(ref, val):
    tm = val.shape[0]
    for s in range(SLAB):
        ref[pl.ds(s, tm, stride=SLAB), :] = val[:, s * LANE:(s + 1) * LANE]


def _from_slab(ref, tm):
    return jnp.concatenate([ref[pl.ds(s, tm, stride=SLAB), :] for s in range(SLAB)], axis=1)


def _moe_route_kernel(x_ref, mod_ref, g_ref, wr_ref, u_ref, mi_ref, mf_ref, cnt_ref, cnt_s):
    i = pl.program_id(0)
    tm = x_ref.shape[0]

    @pl.when(i == 0)
    def _():
        cnt_s[...] = jnp.zeros_like(cnt_s)

    mod = mod_ref[...]
    u = _rms(x_ref[...], g_ref[...]) * (1.0 + mod[4:5]) + mod[3:4]
    _to_slab(u_ref, u)
    logits = jnp.dot(u, wr_ref[...], preferred_element_type=F32, precision=lax.Precision.HIGHEST)
    lane = lax.broadcasted_iota(I32, (tm, LANE), 1)
    lg = jnp.where(lane < N_EXPERTS, logits, -jnp.inf)
    m1 = jnp.max(lg, axis=1, keepdims=True)
    i1 = jnp.min(jnp.where(lg == m1, lane, LANE), axis=1, keepdims=True)
    lg2 = jnp.where(lane == i1, -jnp.inf, lg)
    m2 = jnp.max(lg2, axis=1, keepdims=True)
    i2 = jnp.min(jnp.where(lg2 == m2, lane, LANE), axis=1, keepdims=True)
    e2 = jnp.exp(m2 - m1)
    g1 = 1.0 / (1.0 + e2)
    g2 = e2 / (1.0 + e2)
    oh1 = lane == i1
    oh2 = lane == i2
    cnt = (oh1 | oh2).astype(BF16)
    rr = lax.broadcasted_iota(I32, (tm, tm), 0)
    cc = lax.broadcasted_iota(I32, (tm, tm), 1)
    before = _dot((rr > cc).astype(BF16), cnt) + cnt_s[...]
    r1 = jnp.sum(jnp.where(oh1, before, 0.0), axis=1, keepdims=True).astype(I32)
    r2 = jnp.sum(jnp.where(oh2, before, 0.0), axis=1, keepdims=True).astype(I32)
    cnt_s[...] = cnt_s[...] + jnp.sum(cnt.astype(F32), axis=0, keepdims=True)
    mi_ref[...] = jnp.where(lane == 0, i1, jnp.where(lane == 1, i2, jnp.where(lane == 2, r1, r2)))
    mf_ref[...] = jnp.where(lane == 0, g1, g2)
    cnt_ref[...] = cnt_s[...]


def _moe_route(tl, x, mod, g, w_router):
    T = tl.T
    return pl.pallas_call(
        _moe_route_kernel,
        grid=(tl.n,),
        in_specs=[_row_spec(tl, D_MODEL), _mod_spec(tl), _full((1, D_MODEL)), _full(w_router.shape)],
        out_specs=[pl.BlockSpec((tl.tm * SLAB, LANE), lambda i: (i, 0)),
                   _row_spec(tl, LANE), _row_spec(tl, LANE),
                   pl.BlockSpec((1, LANE), lambda i: (0, 0))],
        out_shape=[
            jax.ShapeDtypeStruct((T * SLAB, LANE), F32),
            jax.ShapeDtypeStruct((T, LANE), I32),
            jax.ShapeDtypeStruct((T, LANE), F32),
            jax.ShapeDtypeStruct((1, LANE), F32),
        ],
        scratch_shapes=[pltpu.VMEM((1, LANE), F32)],
        compiler_params=_cparams(),
        name="moe_route",
    )(x, mod, g, w_router)


DISPATCH_BLK = 512


def _moe_dispatch_kernel(pos_ref, tail_ref, u_ref, z_hbm, xs_hbm, sem, *, nblk, tm, nt):
    i = pl.program_id(0)

    def rows(t, n=1):
        return pl.ds(pl.multiple_of(t * SLAB, SLAB), n * SLAB)

    def issue(r, carry):
        for k in range(2):
            pltpu.make_async_copy(u_ref.at[rows(r)], xs_hbm.at[rows(pos_ref[2 * r + k])],
                                  sem).start(priority=k)
        return carry

    lax.fori_loop(0, nblk, issue, 0, unroll=8)
    for k in range(2):
        pltpu.make_async_copy(u_ref, xs_hbm.at[rows(0, nblk)], sem).wait()

    @pl.when(i == pl.num_programs(0) - 1)
    def _():
        for e in range(N_EXPERTS):
            start = tail_ref[e]
            n = tail_ref[N_EXPERTS + e]

            def fill(r, carry):
                pltpu.make_async_copy(z_hbm.at[rows(0)], xs_hbm.at[rows(start + r)], sem).start()
                return carry

            def drain(r, carry):
                pltpu.make_async_copy(z_hbm.at[rows(0)], xs_hbm.at[rows(start + r)], sem).wait()
                return carry

            lax.fori_loop(0, n, fill, 0)
            lax.fori_loop(0, n, drain, 0)

        def fill_tile(j, carry):
            dst = xs_hbm.at[rows(j * tm, tm)]
            pltpu.make_async_copy(z_hbm, dst, sem).start()
            pltpu.make_async_copy(z_hbm, dst, sem).wait()
            return carry

        lax.fori_loop(tail_ref[2 * N_EXPERTS], nt, fill_tile, 0)


def _moe_dispatch(T, P, tm, pos, tail, u_slab):
    nblk = DISPATCH_BLK
    assert T % nblk == 0 and P % tm == 0
    kern = functools.partial(_moe_dispatch_kernel, nblk=nblk, tm=tm, nt=P // tm)
    zeros = jnp.zeros((tm * SLAB, LANE), F32)
    return pl.pallas_call(
        kern,
        grid=(T // nblk,),
        in_specs=[
            pl.BlockSpec((2 * nblk,), lambda i: (i,), memory_space=pltpu.SMEM),
            pl.BlockSpec(memory_space=pltpu.SMEM),
            pl.BlockSpec((nblk * SLAB, LANE), lambda i: (i, 0)),
            pl.BlockSpec(memory_space=pl.ANY),
        ],
        out_specs=pl.BlockSpec(memory_space=pl.ANY),
        out_shape=jax.ShapeDtypeStruct((P * SLAB, LANE), F32),
        scratch_shapes=[pltpu.SemaphoreType.DMA(())],
        compiler_params=_cparams(),
        name="moe_dispatch",
    )(pos, tail, u_slab, zeros)


def _moe_ffn_kernel(texp_ref, nv_ref, xs_ref, w1_ref, w3_ref, w2_ref, ys_ref):
    j = pl.program_id(0)
    tm = xs_ref.shape[0] // SLAB
    valid = j < nv_ref[0]

    @pl.when(valid)
    def _():
        u = _from_slab(xs_ref, tm).astype(BF16)
        _to_slab(ys_ref, _swiglu(u, w1_ref, w3_ref, w2_ref))

    @pl.when(jnp.logical_not(valid))
    def _():
        ys_ref[...] = jnp.zeros_like(ys_ref)


def _moe_ffn(tm, nt, texp, nv, xs2d, w1, w3, w2):
    wspec = lambda shp: pl.BlockSpec((None,) + shp, lambda j, te, nv_: (te[j], 0, 0))
    grid_spec = pltpu.PrefetchScalarGridSpec(
        num_scalar_prefetch=2,
        grid=(nt,),
        in_specs=[
            pl.BlockSpec((tm * SLAB, LANE), lambda j, te, nv_: (jnp.minimum(j, nv_[0] - 1), 0)),
            wspec(w1.shape[1:]), wspec(w3.shape[1:]), wspec(w2.shape[1:]),
        ],
        out_specs=pl.BlockSpec((tm * SLAB, LANE), lambda j, te, nv_: (j, 0)),
    )
    return pl.pallas_call(
        _moe_ffn_kernel,
        grid_spec=grid_spec,
        out_shape=jax.ShapeDtypeStruct(xs2d.shape, F32),
        compiler_params=_cparams(),
        name="moe_ffn",
    )(texp, nv, xs2d, w1, w3, w2)


def _moe_combine_kernel(pos_ref, x_ref, mod_ref, mf_ref, ys_hbm, o_ref, buf0, buf1, sem):
    tm = x_ref.shape[0]
    bufs = (buf0, buf1)

    def issue(r, carry):
        for k in range(2):
            dst = bufs[k].at[pl.ds(pl.multiple_of(r * SLAB, SLAB), SLAB)]
            pltpu.make_async_copy(ys_hbm.at[pos_ref[2 * r + k]], dst, sem).start(priority=k)
        return carry

    lax.fori_loop(0, tm, issue, 0, unroll=8)
    for k in range(2):
        pltpu.make_async_copy(bufs[1 - k], bufs[k], sem).wait()
    mf = mf_ref[...]
    g1 = mf[:, 0:1]
    g2 = mf[:, 1:2]
    for s in range(SLAB):
        cs = slice(s * LANE, (s + 1) * LANE)
        y = g1 * buf0[pl.ds(s, tm, stride=SLAB), :] + g2 * buf1[pl.ds(s, tm, stride=SLAB), :]
        o_ref[:, cs] = x_ref[:, cs] + mod_ref[5:6, cs] * y


def _moe_combine(tl, pos, x, mod, mf, ys):
    tm = tl.tm
    return pl.pallas_call(
        _moe_combine_kernel,
        grid=(tl.n,),
        in_specs=[
            pl.BlockSpec((2 * tm,), lambda i: (i,), memory_space=pltpu.SMEM),
            _row_spec(tl, D_MODEL), _mod_spec(tl), _row_spec(tl, LANE),
            pl.BlockSpec(memory_space=pl.ANY),
        ],
        out_specs=_row_spec(tl, D_MODEL),
        out_shape=jax.ShapeDtypeStruct((tl.T, D_MODEL), F32),
        scratch_shapes=[pltpu.VMEM((tm * SLAB, LANE), F32), pltpu.VMEM((tm * SLAB, LANE), F32),
                        pltpu.SemaphoreType.DMA(())],
        compiler_params=_cparams(),
        name="moe_combine",
    )(pos, x, mod, mf, ys)


def _moe_layer(tl, x, mod, g, w_router, w1, w3, w2, defer=False):
    T, tm = tl.T, tl.tm
    nt = 2 * tl.n + N_EXPERTS
    P = nt * tm
    wr = jnp.pad(w_router, ((0, 0), (0, LANE - N_EXPERTS)))
    u_slab, mi, mf, cnt = _moe_route(tl, x, mod, g.reshape(1, -1), wr)
    counts = cnt[0, :N_EXPERTS].astype(I32)
    tiles_e = (counts + tm - 1) // tm
    cum_tiles = jnp.cumsum(tiles_e)
    nv = cum_tiles[-1:]
    off = (cum_tiles - tiles_e) * tm
    sel = mi[:, 0:2, None] == jnp.arange(N_EXPERTS, dtype=I32)
    pos = (jnp.sum(jnp.where(sel, off, 0), axis=-1) + mi[:, 2:4]).reshape(2 * T)
    jj = jnp.minimum(jnp.arange(nt, dtype=I32), nv[0] - 1)
    texp = jnp.sum((jj[:, None] >= cum_tiles[None, :]).astype(I32), axis=1)
    tail = jnp.concatenate([off + counts, tiles_e * tm - counts, nv])
    xs = _moe_dispatch(T, P, tm, pos, tail, u_slab)
    ys = _moe_ffn(tm, nt, texp, nv, xs,
                  w1.astype(BF16), w3.astype(BF16), w2.astype(BF16))
    ys = ys.reshape(P, SLAB, LANE)
    if defer:
        return x, (pos, mod, mf, ys)
    return _moe_combine(tl, pos, x, mod, mf, ys), None


def _final_kernel(x_ref, g_ref, o_ref):
    o_ref[...] = _rms(x_ref[...], g_ref[...])


def _final_norm(tl, x, g, first_tile, n_tiles):
    return pl.pallas_call(
        _final_kernel,
        grid=(n_tiles,),
        in_specs=[pl.BlockSpec((tl.tm, D_MODEL), lambda i: (i + first_tile, 0)), _full((1, D_MODEL))],
        out_specs=_row_spec(tl, D_MODEL),
        out_shape=jax.ShapeDtypeStruct((n_tiles * tl.tm, D_MODEL), F32),
        compiler_params=_cparams(),
        name="final_norm",
    )(x, g.reshape(1, -1))


def kernel(x_prompt, x_sample, state_ssd, state_gla, state_lru, c, c_ctx, ada_w, ada_b, norm_g, final_g, ssd_w_in, ssd_conv_w, ssd_conv_b, ssd_a_log, ssd_dt_bias, ssd_d, ssd_norm_g, ssd_w_out, gla_w_in, gla_w_gate_up, gla_b_gate, gla_norm_g, gla_w_out, lru_w_in, lru_conv_w, lru_conv_b, lru_w_a, lru_b_a, lru_w_x, lru_b_x, lru_lambda, lru_w_out, ffn_w1, ffn_w3, ffn_w2, moe_router, moe_w1, moe_w3, moe_w2):
    n_ctx, len_ctx, _ = x_prompt.shape
    n_lat, len_lat, _ = x_sample.shape
    lay = Layout(n_ctx, len_ctx, n_lat, len_lat)
    tl = _Tiles(lay, TM)
    tl2 = tl.wide
    assert len_ctx & (len_ctx - 1) == 0 and len_ctx <= TM and TM % GRID_W == 0

    x = jnp.concatenate([x_prompt.reshape(-1, D_MODEL), x_sample.reshape(-1, D_MODEL)], axis=0)
    c_all = jnp.concatenate([c_ctx[None], c], axis=0)
    c_all = jnp.pad(c_all, ((0, -c_all.shape[0] % SUBLANE), (0, 0)))
    mods = _mods(c_all, ada_w, ada_b)

    caches = (state_ssd, state_gla, state_lru)
    ctx_states = ([], [], [])
    pending = None
    for i in range(DEPTH):
        kind, j = i % N_MIXERS, i // N_MIXERS
        cache = caches[kind][:, j]
        h0 = jnp.concatenate([jnp.zeros((n_ctx,) + cache.shape[1:], F32), cache], axis=0)
        dense = (norm_g[i, 1], ffn_w1[i // 2], ffn_w3[i // 2], ffn_w2[i // 2]) if i % 2 == 0 else None
        if kind == 0:
            x, hfin = _ssd_layer(tl, lay, x, mods[i], norm_g[i, 0], h0, ssd_w_in[j], ssd_conv_w[j],
                                 ssd_conv_b[j], ssd_a_log[j], ssd_dt_bias[j], ssd_d[j],
                                 ssd_norm_g[j], ssd_w_out[j], ffn=dense)
        elif kind == 1:
            x, hfin = _gla_layer(tl, lay, x, mods[i], norm_g[i, 0], h0, gla_w_in[j], gla_w_gate_up[j],
                                 gla_b_gate[j], gla_norm_g[j], gla_w_out[j])
        else:
            x, hfin = _lru_layer(tl, lay, x, mods[i], norm_g[i, 0], h0, lru_w_in[j], lru_conv_w[j],
                                 lru_conv_b[j], lru_w_a[j], lru_b_a[j], lru_w_x[j], lru_b_x[j],
                                 lru_lambda[j], lru_w_out[j], ffn=dense, pending=pending)
            pending = None
        assert pending is None
        ctx_states[kind].append(hfin[:n_ctx])
        if i % 2 == 0:
            if kind == 1:
                x = _ffn_layer(tl, x, mods[i], *dense)
        else:
            next_is_lru = i + 1 < DEPTH and (i + 1) % N_MIXERS == 2
            x, pending = _moe_layer(tl, x, mods[i], norm_g[i, 1], moe_router[i // 2], moe_w1[i // 2],
                                    moe_w3[i // 2], moe_w2[i // 2], defer=next_is_lru)

    y_ctx = _final_norm(tl2, x, final_g, 0, tl2.nct)
    y_lat = _final_norm(tl2, x, final_g, tl2.nct, tl2.n - tl2.nct)
    return (y_ctx.reshape(n_ctx, len_ctx, D_MODEL),
            y_lat.reshape(n_lat, len_lat, D_MODEL),
            jnp.stack(ctx_states[0], axis=1),
            jnp.stack(ctx_states[1], axis=1),
            jnp.stack(ctx_states[2], axis=1))
```

```python
import collections
import functools

import jax
import jax.numpy as jnp
import numpy as np
from jax import lax
from jax.experimental import pallas as pl
from jax.experimental.pallas import tpu as pltpu

F32 = jnp.float32
BF16 = jnp.bfloat16
I32 = jnp.int32

D_MODEL = 1024
DEPTH = 4
GRID_W = 64
N_MIXERS = 3
EPS = 1e-6
CONV_W = 4

SSD_INNER = 2 * D_MODEL
SSD_HEAD_DIM = 64
SSD_HEADS = SSD_INNER // SSD_HEAD_DIM
SSD_GROUPS = 8
SSD_GROUP_HEADS = SSD_HEADS // SSD_GROUPS
SSD_GROUP_W = SSD_GROUP_HEADS * SSD_HEAD_DIM
SSD_STATE = 128
SSD_CHUNK = 128
SSD_BC = SSD_GROUPS * SSD_STATE
SSD_CONV_CH = SSD_INNER + 2 * SSD_BC

GLA_HEADS = 4
GLA_KEY = D_MODEL // 2
GLA_VAL = D_MODEL
GLA_DK = GLA_KEY // GLA_HEADS
GLA_DV = GLA_VAL // GLA_HEADS
GLA_RANK = 16
GLA_TAU = 16.0
GLA_CHUNK = 64

LRU_WIDTH = D_MODEL
LRU_BLOCKS = 8
LRU_BLOCK = LRU_WIDTH // LRU_BLOCKS
LRU_C = 8.0

D_FF = 2816
N_EXPERTS = 8

LOG2E = 1.4426950408889634

LANE = 128
SUBLANE = 8
TM = 256
TM_WIDE = 512
SCAN_BLK = 256
VMEM_LIMIT = 56 * 1024 * 1024

Layout = collections.namedtuple("Layout", "n_ctx len_ctx n_lat len_lat")


def _cparams(n_axes=1):
    return pltpu.CompilerParams(
        dimension_semantics=("arbitrary",) * n_axes, vmem_limit_bytes=VMEM_LIMIT)


def _silu(x):
    return x * jax.nn.sigmoid(x)


def _softplus(x):
    return jnp.maximum(x, 0.0) + jnp.log(1.0 + jnp.exp(-jnp.abs(x)))


def _rms(x, g):
    return x * lax.rsqrt(jnp.mean(x * x, axis=-1, keepdims=True) + EPS) * g


def _dot(a, b):
    return jnp.dot(a, b, preferred_element_type=F32)


def _dot_nt(a, b):
    return lax.dot_general(a, b, (((1,), (1,)), ((), ())), preferred_element_type=F32)


def _dot_tn(a, b):
    return lax.dot_general(a, b, (((0,), (0,)), ((), ())), preferred_element_type=F32)


def _split3(x):
    hi = x.astype(BF16)
    r1 = x - hi.astype(F32)
    mid = r1.astype(BF16)
    lo = (r1 - mid.astype(F32)).astype(BF16)
    return hi, mid, lo


def _tri_cumsum(tri, x):
    hi, mid, lo = _split3(x)
    return _dot(tri, hi) + _dot(tri, mid) + _dot(tri, lo)


class _Tiles:
    def __init__(self, lay, tm):
        assert (lay.n_ctx * lay.len_ctx) % tm == 0 and lay.len_lat % tm == 0
        self.lay = lay
        self.tm = tm
        self.nct = lay.n_ctx * lay.len_ctx // tm
        self.tpl = lay.len_lat // tm
        self.n = self.nct + lay.n_lat * self.tpl
        self.T = self.n * tm

    def mod_row(self, i):
        return jnp.where(i < self.nct, 0, 1 + (i - self.nct) // self.tpl)

    @property
    def wide(self):
        return _Tiles(self.lay, TM_WIDE)


def _full(shape):
    nd = len(shape)
    return pl.BlockSpec(shape, lambda *_: (0,) * nd, pipeline_mode=pl.Buffered(1))


def _mod_spec(tl):
    return pl.BlockSpec((None, 6, D_MODEL), lambda i, *_: (tl.mod_row(i), 0, 0))


def _row_spec(tl, width):
    return pl.BlockSpec((tl.tm, width), lambda i, *_: (i, 0))


def _mods_kernel(c_ref, w_ref, b_ref, o_ref):
    c = c_ref[...]
    a = _silu(c)
    o_ref[...] = jnp.dot(a, w_ref[...], preferred_element_type=F32,
                         precision=lax.Precision.HIGHEST) + b_ref[...]


def _mods(c_all, ada_w, ada_b):
    R = c_all.shape[0]
    nb = 6
    out = pl.pallas_call(
        _mods_kernel,
        grid=(DEPTH, nb),
        in_specs=[
            pl.BlockSpec((R, D_MODEL), lambda l, j: (0, 0)),
            pl.BlockSpec((None, D_MODEL, D_MODEL), lambda l, j: (l, 0, j)),
            pl.BlockSpec((None, 1, D_MODEL), lambda l, j: (l, 0, j)),
        ],
        out_specs=pl.BlockSpec((None, R, D_MODEL), lambda l, j: (l, 0, j)),
        out_shape=jax.ShapeDtypeStruct((DEPTH, R, 6 * D_MODEL), F32),
        compiler_params=_cparams(2),
        name="mods",
    )(c_all, ada_w, ada_b.reshape(DEPTH, 1, 6 * D_MODEL))
    return out.reshape(DEPTH, R, 6, D_MODEL)


CONV_OFFSETS = (-1, 0, 1, 2)


def _shift_mats(tm, segs):
    t = np.arange(tm)
    out = []
    for seg in segs:
        mats = []
        for off in CONV_OFFSETS:
            if off == 0:
                continue
            src = t + off
            ok = (src >= 0) & (src < tm) & (src // seg == t // seg)
            m = np.zeros((tm, tm), np.float32)
            m[t[ok], src[ok]] = 1.0
            mats.append(m)
        out.append(np.concatenate(mats, axis=1))
    return jnp.asarray(np.stack(out), BF16)


def _shift_spec(tl):
    return pl.BlockSpec((None, tl.tm, 3 * tl.tm), lambda i: (jnp.where(i < tl.nct, 0, 1), 0, 0))


def _conv4(p, cw, cb, shifts):
    taps = [(p * cw[k:k + 1]).astype(BF16) for k in (0, 2, 3)]
    return cb + cw[1:2] * p + _dot(shifts, jnp.concatenate(taps, axis=0))


def _ssd_in_kernel(x_ref, mod_ref, g_ref, w_ref, wdt_ref, cw_ref, cb_ref, dtb_ref, sh_ref,
                   z_ref, xs_ref, b_ref, c_ref, dt_ref):
    mod = mod_ref[...]
    u = (_rms(x_ref[...], g_ref[...]) * (1.0 + mod[1:2]) + mod[0:1]).astype(BF16)
    z_ref[...] = _dot(u, w_ref[:, :SSD_INNER]).astype(z_ref.dtype)
    shifts = sh_ref[...]
    ch = 512
    for c0 in range(0, SSD_CONV_CH, ch):
        p = _dot(u, w_ref[:, SSD_INNER + c0:SSD_INNER + c0 + ch])
        y = _silu(_conv4(p, cw_ref[:, c0:c0 + ch], cb_ref[:, c0:c0 + ch], shifts))
        if c0 < SSD_INNER:
            dst, base, w = xs_ref, c0, SSD_GROUP_W
        elif c0 < SSD_INNER + SSD_BC:
            dst, base, w = b_ref, c0 - SSD_INNER, SSD_STATE
        else:
            dst, base, w = c_ref, c0 - SSD_INNER - SSD_BC, SSD_STATE
        for k in range(ch // w):
            dst[base // w + k] = y[:, k * w:(k + 1) * w].astype(dst.dtype)
    dt_ref[...] = _softplus(_dot(u, wdt_ref[...]) + dtb_ref[...])


def _ssd_in(tl, lay, x, mod, g, w_main, w_dt, conv_w, conv_b, dt_bias):
    T = tl.T
    gspec = lambda w: pl.BlockSpec((SSD_GROUPS, tl.tm, w), lambda i: (0, i, 0))
    return pl.pallas_call(
        _ssd_in_kernel,
        grid=(tl.n,),
        in_specs=[
            _row_spec(tl, D_MODEL), _mod_spec(tl), _full((1, D_MODEL)),
            _full(w_main.shape), _full(w_dt.shape), _full(conv_w.shape),
            _full(conv_b.shape), _full(dt_bias.shape), _shift_spec(tl),
        ],
        out_specs=[
            _row_spec(tl, SSD_INNER), gspec(SSD_GROUP_W), gspec(SSD_STATE), gspec(SSD_STATE),
            _row_spec(tl, 2 * LANE),
        ],
        out_shape=[
            jax.ShapeDtypeStruct((T, SSD_INNER), BF16),
            jax.ShapeDtypeStruct((SSD_GROUPS, T, SSD_GROUP_W), BF16),
            jax.ShapeDtypeStruct((SSD_GROUPS, T, SSD_STATE), BF16),
            jax.ShapeDtypeStruct((SSD_GROUPS, T, SSD_STATE), BF16),
            jax.ShapeDtypeStruct((T, 2 * LANE), F32),
        ],
        compiler_params=_cparams(),
        name="ssd_in",
    )(x, mod, g, w_main, w_dt, conv_w, conv_b, dt_bias, _shift_mats(tl.tm, (lay.len_ctx, GRID_W)))


def _ssd_scan_kernel(bc_ref, seq_ref, first_ref, last_ref,
                     xf_ref, bf_ref, cf_ref, dtf_ref, xb_ref, bb_ref, cb_ref, dtb_ref,
                     alog_ref, h0_ref, yf_ref, yb_ref, hout_ref, h_s, cum_s, crow_s, wrow_s):
    i = pl.program_id(0)
    Q = SSD_CHUNK
    P = SSD_HEAD_DIM
    pairs_per_group = SSD_GROUP_HEADS // 2

    @pl.when(first_ref[i] == 1)
    def _():
        h_s[...] = h0_ref[...]

    rows = lax.broadcasted_iota(I32, (Q, Q), 0)
    cols = lax.broadcasted_iota(I32, (Q, Q), 1)
    masks = (rows >= cols, rows <= cols)
    first_head = lax.broadcasted_iota(I32, (1, LANE), 1) < P
    dirs = ((xf_ref, bf_ref, cf_ref, dtf_ref, yf_ref), (xb_ref, bb_ref, cb_ref, dtb_ref, yb_ref))

    for d in range(2):
        x_ref, b_ref, c_ref, dt_ref, y_ref = dirs[d]
        end = Q - 1 if d == 0 else 0
        dt = dt_ref[...]
        cum = _tri_cumsum(masks[d].astype(BF16), -dt * jnp.exp(alog_ref[d:d + 1]))
        cum_t = cum.T
        dt_t = dt.T
        cum_s[d] = cum * LOG2E
        crow_s[d] = (cum_t - jnp.log(dt_t)) * LOG2E
        wrow_s[d] = dt_t * jnp.exp(cum_t[:, end:end + 1] - cum_t)
        cbs = [_dot_nt(c_ref[g], b_ref[g]) for g in range(SSD_GROUPS)]
        for g in range(SSD_GROUPS):
            bg = b_ref[g]
            cg = c_ref[g]
            cb = cbs[g]
            cgf = cg.astype(F32)
            bgt = bg.astype(F32).T
            for j in range(pairs_per_group):
                p = g * pairs_per_group + j
                xp = x_ref[g, :, j * LANE:(j + 1) * LANE]
                zero = jnp.zeros_like(xp)
                xa = jnp.where(first_head, xp, zero)
                xb = jnp.where(first_head, zero, xp)
                hp = h_s[d, p]
                hpb = hp.astype(BF16)
                ha = jnp.where(first_head, hpb, jnp.zeros_like(hpb))
                hb = jnp.where(first_head, jnp.zeros_like(hpb), hpb)
                lhs, bws, tots = [], [], []
                for h in (2 * p, 2 * p + 1):
                    ccol = jnp.broadcast_to(cum_s[d, :, h:h + 1], (Q, Q))
                    ecol = jnp.exp2(ccol)
                    dec = jnp.where(masks[d], jnp.exp2(ccol - crow_s[d, h:h + 1, :]), 0.0)
                    lhs += [(cb * dec).astype(BF16), (cgf * ecol).astype(BF16)]
                    bws.append((bgt * wrow_s[d, h:h + 1, :]).astype(BF16))
                    tots.append(ecol[end:end + 1])
                y = _dot(jnp.concatenate(lhs, axis=1), jnp.concatenate([xa, ha, xb, hb], axis=0))
                y_ref[g, :, j * LANE:(j + 1) * LANE] = y.astype(y_ref.dtype)
                upd = _dot(jnp.concatenate(bws, axis=1), jnp.concatenate([xa, xb], axis=0))
                h_s[d, p] = hp * jnp.where(first_head, tots[0], tots[1]) + upd

    @pl.when(last_ref[i] == 1)
    def _():
        hout_ref[...] = h_s[...]


def _scan_tables(lay, blk):
    bc, seq, first, last = [], [], [], []
    s0 = 0
    sid = 0
    for n, L in ((lay.n_ctx, lay.len_ctx), (lay.n_lat, lay.len_lat)):
        nb = L // blk
        for _ in range(n):
            for k in range(nb):
                bc.append(s0 + nb - 1 - k)
                seq.append(sid)
                first.append(int(k == 0))
                last.append(int(k == nb - 1))
            s0 += nb
            sid += 1
    return tuple(jnp.asarray(np.asarray(a, np.int32)) for a in (bc, seq, first, last))


def _ssd_scan(lay, xs, bm, cm, dt, a_log, h0):
    Q = SSD_CHUNK
    T = xs.shape[1]
    tabs = _scan_tables(lay, Q)
    nseq = lay.n_ctx + lay.n_lat
    G, N, W = SSD_GROUPS, SSD_STATE, SSD_GROUP_W
    st_shape = (2, SSD_HEADS // 2, N, 2 * SSD_HEAD_DIM)
    assert 2 * SSD_HEAD_DIM == LANE and Q == LANE
    fwd = lambda w: pl.BlockSpec((G, Q, w), lambda i, bc, sq, fi, la: (0, i, 0))
    bwd = lambda w: pl.BlockSpec((G, Q, w), lambda i, bc, sq, fi, la: (0, bc[i], 0))
    st_spec = pl.BlockSpec((None,) + st_shape, lambda i, bc, sq, fi, la: (sq[i], 0, 0, 0, 0))
    grid_spec = pltpu.PrefetchScalarGridSpec(
        num_scalar_prefetch=4,
        grid=(T // Q,),
        in_specs=[
            fwd(W), fwd(N), fwd(N), pl.BlockSpec((Q, LANE), lambda i, bc, sq, fi, la: (i, 0)),
            bwd(W), bwd(N), bwd(N), pl.BlockSpec((Q, LANE), lambda i, bc, sq, fi, la: (bc[i], 1)),
            pl.BlockSpec((2, LANE), lambda i, *_: (0, 0)),
            st_spec,
        ],
        out_specs=[fwd(W), bwd(W), st_spec],
        scratch_shapes=[pltpu.VMEM(st_shape, F32), pltpu.VMEM((2, Q, LANE), F32),
                        pltpu.VMEM((2, LANE, Q), F32), pltpu.VMEM((2, LANE, Q), F32)],
    )
    return pl.pallas_call(
        _ssd_scan_kernel,
        grid_spec=grid_spec,
        out_shape=[
            jax.ShapeDtypeStruct((G, T, W), BF16),
            jax.ShapeDtypeStruct((G, T, W), BF16),
            jax.ShapeDtypeStruct((nseq,) + st_shape, F32),
        ],
        compiler_params=_cparams(),
        name="ssd_scan",
    )(*tabs, xs, bm, cm, dt, xs, bm, cm, dt, a_log, h0)


def _ffn_tail(x, mod, ffn_refs):
    if not ffn_refs:
        return x
    g_ref, w1_ref, w3_ref, w2_ref = ffn_refs
    u = (_rms(x, g_ref[...]) * (1.0 + mod[4:5]) + mod[3:4]).astype(BF16)
    return x + mod[5:6] * _swiglu(u, w1_ref, w3_ref, w2_ref)


def _ffn_operands(ffn):
    if ffn is None:
        return [], []
    g, w1, w3, w2 = ffn
    ops = [g.reshape(1, -1), w1.astype(BF16), w3.astype(BF16), w2.astype(BF16)]
    return ops, [_full(o.shape) for o in ops]


def _ssd_out_kernel(yf_ref, yb_ref, xs_ref, z_ref, x_ref, mod_ref, d_ref, ng_ref, w_ref, *rest):
    *ffn_refs, o_ref = rest
    mod = mod_ref[...]
    parts = []
    for g in range(SSD_GROUPS):
        dg = d_ref[:, g * SSD_GROUP_W:(g + 1) * SSD_GROUP_W]
        parts.append(yf_ref[g].astype(F32) + yb_ref[g].astype(F32) + dg * xs_ref[g].astype(F32))
    y = jnp.concatenate(parts, axis=1)
    y = _rms(y * _silu(z_ref[...].astype(F32)), ng_ref[...]).astype(BF16)
    o_ref[...] = _ffn_tail(x_ref[...] + mod[2:3] * _dot(y, w_ref[...]), mod, ffn_refs)


def _ssd_out(tl, yf, yb, xs, z, x, mod, d_exp, norm_g, w_out, ffn=None):
    tl = tl if ffn is not None else tl.wide
    ffn_ops, ffn_specs = _ffn_operands(ffn)
    gspec = pl.BlockSpec((SSD_GROUPS, tl.tm, SSD_GROUP_W), lambda i: (0, i, 0))
    return pl.pallas_call(
        _ssd_out_kernel,
        grid=(tl.n,),
        in_specs=[gspec, gspec, gspec, _row_spec(tl, SSD_INNER), _row_spec(tl, D_MODEL),
                  _mod_spec(tl), _full(d_exp.shape), _full(norm_g.shape), _full(w_out.shape)] + ffn_specs,
        out_specs=_row_spec(tl, D_MODEL),
        out_shape=jax.ShapeDtypeStruct((tl.T, D_MODEL), F32),
        compiler_params=_cparams(),
        name="ssd_out",
    )(yf, yb, xs, z, x, mod, d_exp, norm_g, w_out, *ffn_ops)


def _ssd_layer(tl, lay, x, mod, g, h0, w_in, conv_w, conv_b, a_log, dt_bias, d_skip, norm_g, w_out,
               ffn=None):
    nseq = h0.shape[0]
    split = SSD_INNER + SSD_CONV_CH
    w_main = w_in[:, :split].astype(BF16)
    lane_pad = lambda v: jnp.pad(v, ((0, 0), (0, 0), (0, LANE - SSD_HEADS)))
    w_dt = lane_pad(w_in[:, split:].reshape(D_MODEL, 2, SSD_HEADS)).reshape(D_MODEL, 2 * LANE).astype(BF16)
    dtb = lane_pad(dt_bias.reshape(1, 2, SSD_HEADS)).reshape(1, 2 * LANE)
    alog = lane_pad(a_log.reshape(1, 2, SSD_HEADS)).reshape(2, LANE)
    z, xs, bm, cm, dt = _ssd_in(tl, lay, x, mod, g.reshape(1, -1), w_main, w_dt,
                                conv_w, conv_b.reshape(1, -1), dtb)
    pair_shape = (nseq, 2, SSD_HEADS // 2, 2 * SSD_HEAD_DIM, SSD_STATE)
    yf, yb, hfin = _ssd_scan(lay, xs, bm, cm, dt, alog, h0.reshape(pair_shape).swapaxes(-1, -2))
    hfin = hfin.swapaxes(-1, -2).reshape(h0.shape)
    d_exp = jnp.repeat(d_skip, SSD_HEAD_DIM).reshape(1, -1)
    x = _ssd_out(tl, yf, yb, xs, z, x, mod, d_exp, norm_g.reshape(1, -1), w_out.astype(BF16), ffn)
    return x, hfin


def _gla_in_kernel(x_ref, mod_ref, g_ref, w_ref, wgd_ref, wup_ref, bg_ref,
                   q_ref, k_ref, v_ref, r_ref, la_ref):
    mod = mod_ref[...]
    u = (_rms(x_ref[...], g_ref[...]) * (1.0 + mod[1:2]) + mod[0:1]).astype(BF16)
    q_ref[...] = (_dot(u, w_ref[:, :GLA_KEY]) * (GLA_DK ** -0.5)).astype(q_ref.dtype)
    k_ref[...] = _dot(u, w_ref[:, GLA_KEY:2 * GLA_KEY]).astype(k_ref.dtype)
    v_ref[...] = _dot(u, w_ref[:, 2 * GLA_KEY:2 * GLA_KEY + GLA_VAL]).astype(v_ref.dtype)
    r_ref[...] = _dot(u, w_ref[:, 2 * GLA_KEY + GLA_VAL:]).astype(r_ref.dtype)
    gd = _dot(u, wgd_ref[...]).astype(BF16)
    logit = _dot(gd, wup_ref[...]) + bg_ref[...]
    la_ref[...] = (jnp.minimum(logit, 0.0) - jnp.log(1.0 + jnp.exp(-jnp.abs(logit)))) / GLA_TAU


def _gla_in(tl, x, mod, g, w_main, w_gd, w_up, b_gate):
    tl = tl.wide
    T = tl.T
    return pl.pallas_call(
        _gla_in_kernel,
        grid=(tl.n,),
        in_specs=[_row_spec(tl, D_MODEL), _mod_spec(tl), _full((1, D_MODEL)), _full(w_main.shape),
                  _full(w_gd.shape), _full(w_up.shape), _full(b_gate.shape)],
        out_specs=[_row_spec(tl, GLA_KEY), _row_spec(tl, GLA_KEY), _row_spec(tl, GLA_VAL),
                   _row_spec(tl, GLA_VAL), _row_spec(tl, 2 * GLA_KEY)],
        out_shape=[
            jax.ShapeDtypeStruct((T, GLA_KEY), BF16),
            jax.ShapeDtypeStruct((T, GLA_KEY), BF16),
            jax.ShapeDtypeStruct((T, GLA_VAL), BF16),
            jax.ShapeDtypeStruct((T, GLA_VAL), BF16),
            jax.ShapeDtypeStruct((T, 2 * GLA_KEY), F32),
        ],
        compiler_params=_cparams(),
        name="gla_in",
    )(x, mod, g, w_main, w_gd, w_up, b_gate)


def _gla_scan_kernel(bc_ref, seq_ref, first_ref, last_ref,
                     qf_ref, kf_ref, vf_ref, laf_ref, qb_ref, kb_ref, vb_ref, lab_ref, s0_ref,
                     of_ref, ob_ref, sout_ref, s_s):
    i = pl.program_id(0)
    C = GLA_CHUNK
    nsub = qf_ref.shape[0] // C

    @pl.when(first_ref[i] == 1)
    def _():
        s_s[...] = s0_ref[...]

    nrow = qf_ref.shape[0]
    rows = lax.broadcasted_iota(I32, (nrow, nrow), 0)
    cols = lax.broadcasted_iota(I32, (nrow, nrow), 1)
    shift = C.bit_length() - 1
    assert C == 1 << shift
    same_chunk = jnp.right_shift(rows, shift) == jnp.right_shift(cols, shift)
    tris = (jnp.where(same_chunk & (rows >= cols), 1.0, 0.0).astype(BF16),
            jnp.where(same_chunk & (rows <= cols), 1.0, 0.0).astype(BF16))
    rows_c = lax.broadcasted_iota(I32, (C, C), 0)
    cols_c = lax.broadcasted_iota(I32, (C, C), 1)
    masks = (rows_c >= cols_c, rows_c <= cols_c)
    dirs = ((qf_ref, kf_ref, vf_ref, laf_ref, of_ref), (qb_ref, kb_ref, vb_ref, lab_ref, ob_ref))

    for d in range(2):
        q_ref, k_ref, v_ref, la_ref, o_ref = dirs[d]
        gcum_all = _tri_cumsum(tris[d], la_ref[:, d * GLA_KEY:(d + 1) * GLA_KEY])
        for c in range(nsub):
            cc = c if d == 0 else nsub - 1 - c
            rs = slice(cc * C, (cc + 1) * C)
            for h in range(GLA_HEADS):
                ks = slice(h * GLA_DK, (h + 1) * GLA_DK)
                vs = slice(h * GLA_DV, (h + 1) * GLA_DV)
                gcum = gcum_all[rs, ks]
                qh = q_ref[rs, ks].astype(F32)
                kh = k_ref[rs, ks].astype(F32)
                vh = v_ref[rs, vs]
                qi = (qh * jnp.exp(gcum)).astype(BF16)
                ki = (kh * jnp.exp(-gcum)).astype(BF16)
                att = jnp.where(masks[d], _dot_nt(qi, ki), 0.0).astype(BF16)
                st = s_s[d, h]
                o_ref[rs, vs] = (_dot(att, vh) + _dot_nt(qi, st.astype(BF16))).astype(o_ref.dtype)
                glast = gcum[C - 1:C] if d == 0 else gcum[0:1]
                kdec = (kh * jnp.exp(glast - gcum)).astype(BF16)
                s_s[d, h] = st * jnp.exp(glast) + _dot_tn(vh, kdec)

    @pl.when(last_ref[i] == 1)
    def _():
        sout_ref[...] = s_s[...]


def _gla_scan(lay, q, k, v, la, s0):
    B = SCAN_BLK
    T = q.shape[0]
    tabs = _scan_tables(lay, B)
    nseq = lay.n_ctx + lay.n_lat
    fwd = lambda w: pl.BlockSpec((B, w), lambda i, bc, sq, fi, la_: (i, 0))
    bwd = lambda w: pl.BlockSpec((B, w), lambda i, bc, sq, fi, la_: (bc[i], 0))
    st_spec = pl.BlockSpec((None, 2, GLA_HEADS, GLA_DV, GLA_DK),
                           lambda i, bc, sq, fi, la_: (sq[i], 0, 0, 0, 0))
    grid_spec = pltpu.PrefetchScalarGridSpec(
        num_scalar_prefetch=4,
        grid=(T // B,),
        in_specs=[fwd(GLA_KEY), fwd(GLA_KEY), fwd(GLA_VAL), fwd(2 * GLA_KEY),
                  bwd(GLA_KEY), bwd(GLA_KEY), bwd(GLA_VAL), bwd(2 * GLA_KEY), st_spec],
        out_specs=[fwd(GLA_VAL), bwd(GLA_VAL), st_spec],
        scratch_shapes=[pltpu.VMEM((2, GLA_HEADS, GLA_DV, GLA_DK), F32)],
    )
    return pl.pallas_call(
        _gla_scan_kernel,
        grid_spec=grid_spec,
        out_shape=[
            jax.ShapeDtypeStruct((T, GLA_VAL), BF16),
            jax.ShapeDtypeStruct((T, GLA_VAL), BF16),
            jax.ShapeDtypeStruct((nseq, 2, GLA_HEADS, GLA_DV, GLA_DK), F32),
        ],
        compiler_params=_cparams(),
        name="gla_scan",
    )(*tabs, q, k, v, la, q, k, v, la, s0)


def _gla_out_kernel(of_ref, ob_ref, r_ref, x_ref, mod_ref, ng_ref, w_ref, o_ref):
    o = of_ref[...].astype(F32) + ob_ref[...].astype(F32)
    ng = ng_ref[...]
    parts = [_rms(o[:, h * GLA_DV:(h + 1) * GLA_DV], ng) for h in range(GLA_HEADS)]
    y = (jnp.concatenate(parts, axis=1) * _silu(r_ref[...].astype(F32))).astype(BF16)
    o_ref[...] = x_ref[...] + mod_ref[2:3] * _dot(y, w_ref[...])


def _gla_out(tl, of, ob, r, x, mod, norm_g, w_out):
    tl = tl.wide
    return pl.pallas_call(
        _gla_out_kernel,
        grid=(tl.n,),
        in_specs=[_row_spec(tl, GLA_VAL), _row_spec(tl, GLA_VAL), _row_spec(tl, GLA_VAL),
                  _row_spec(tl, D_MODEL), _mod_spec(tl), _full(norm_g.shape), _full(w_out.shape)],
        out_specs=_row_spec(tl, D_MODEL),
        out_shape=jax.ShapeDtypeStruct((tl.T, D_MODEL), F32),
        compiler_params=_cparams(),
        name="gla_out",
    )(of, ob, r, x, mod, norm_g, w_out)


def _gla_layer(tl, lay, x, mod, g, s0, w_in, w_gate_up, b_gate, norm_g, w_out):
    split = 2 * GLA_KEY + 2 * GLA_VAL
    w_main = w_in[:, :split].astype(BF16)
    w_gd = jnp.pad(w_in[:, split:], ((0, 0), (0, LANE - 2 * GLA_RANK))).astype(BF16)
    w_up = jnp.zeros((LANE, 2 * GLA_KEY), F32)
    for d in range(2):
        w_up = w_up.at[d * GLA_RANK:(d + 1) * GLA_RANK, d * GLA_KEY:(d + 1) * GLA_KEY].set(w_gate_up[d])
    q, k, v, r, la = _gla_in(tl, x, mod, g.reshape(1, -1), w_main, w_gd, w_up.astype(BF16),
                             b_gate.reshape(1, -1))
    of, ob, sfin = _gla_scan(lay, q, k, v, la, s0.swapaxes(-1, -2))
    x = _gla_out(tl, of, ob, r, x, mod, norm_g.reshape(1, -1), w_out.astype(BF16))
    return x, sfin.swapaxes(-1, -2)


def _gelu_tanh(x):
    return 0.5 * x * (1.0 + jnp.tanh(0.7978845608028654 * (x + 0.044715 * (x * x * x))))


def _lru_in_kernel(*refs, pending):
    if pending:
        (pos_ref, posn_ref, modp_ref, mf_ref, ys_hbm, x_ref, mod_ref, g_ref, w_ref, cw_ref, cb_ref,
         wg_ref, bg_ref, lam_ref, sh_ref, a_ref, b_ref, gg_ref, xo_ref, buf, sem) = refs
        i = pl.program_id(0)
        tm = x_ref.shape[0]

        def gather(p_ref, slot):
            def issue(r, carry):
                for k in range(2):
                    dst = buf.at[slot, k, pl.ds(pl.multiple_of(r * SLAB, SLAB), SLAB)]
                    pltpu.make_async_copy(ys_hbm.at[p_ref[2 * r + k]], dst, sem.at[slot]).start(priority=k)
                return carry
            lax.fori_loop(0, tm, issue, 0, unroll=8)

        @pl.when(i == 0)
        def _():
            gather(pos_ref, 0)

        for slot in range(2):
            @pl.when((i % 2 == slot) & (i + 1 < pl.num_programs(0)))
            def _():
                gather(posn_ref, 1 - slot)

            @pl.when(i % 2 == slot)
            def _():
                for k in range(2):
                    pltpu.make_async_copy(buf.at[slot, 1 - k], buf.at[slot, k], sem.at[slot]).wait()
                mf = mf_ref[...]
                for s in range(SLAB):
                    cs = slice(s * LANE, (s + 1) * LANE)
                    y = (mf[:, 0:1] * buf[slot, 0, pl.ds(s, tm, stride=SLAB), :]
                         + mf[:, 1:2] * buf[slot, 1, pl.ds(s, tm, stride=SLAB), :])
                    xo_ref[:, cs] = x_ref[:, cs] + modp_ref[5:6, cs] * y

        x = xo_ref[...]
    else:
        (x_ref, mod_ref, g_ref, w_ref, cw_ref, cb_ref, wg_ref, bg_ref, lam_ref, sh_ref,
         a_ref, b_ref, gg_ref) = refs
        x = x_ref[...]
    W = LRU_WIDTH
    mod = mod_ref[...]
    u = (_rms(x, g_ref[...]) * (1.0 + mod[1:2]) + mod[0:1]).astype(BF16)
    gg_ref[...] = _gelu_tanh(_dot(u, w_ref[:, :W])).astype(gg_ref.dtype)
    xb = _conv4(_dot(u, w_ref[:, W:]), cw_ref[...], cb_ref[...], sh_ref[...])
    sp = _softplus(-lam_ref[...])
    for n in range(LRU_BLOCKS):
        cs = slice(n * LRU_BLOCK, (n + 1) * LRU_BLOCK)
        xn = xb[:, cs]
        gates = jax.nn.sigmoid(_dot(xn.astype(BF16), wg_ref[n]) + bg_ref[n])
        for d in range(2):
            r = gates[:, d * LRU_BLOCK:(d + 1) * LRU_BLOCK]
            ig = gates[:, (2 + d) * LRU_BLOCK:(3 + d) * LRU_BLOCK]
            log_a = (-LRU_C) * r * sp[:, d * W + n * LRU_BLOCK:d * W + (n + 1) * LRU_BLOCK]
            a = jnp.exp(log_a)
            ds = slice(d * W + n * LRU_BLOCK, d * W + (n + 1) * LRU_BLOCK)
            a_ref[:, ds] = a
            b_ref[:, ds] = jnp.sqrt(1.0 - a * a) * (ig * xn)


def _lru_in(tl, lay, x, mod, g, w_in, conv_w, conv_b, w_gates, b_gates, lam, pending=None):
    T, tm = tl.T, tl.tm
    operands = [x, mod, g, w_in, conv_w, conv_b, w_gates, b_gates, lam,
                _shift_mats(tm, (lay.len_ctx, GRID_W))]
    in_specs = [_row_spec(tl, D_MODEL), _mod_spec(tl), _full((1, D_MODEL)), _full(w_in.shape),
                _full(conv_w.shape), _full(conv_b.shape), _full(w_gates.shape),
                _full(b_gates.shape), _full(lam.shape), _shift_spec(tl)]
    out_specs = [_row_spec(tl, 2 * LRU_WIDTH), _row_spec(tl, 2 * LRU_WIDTH), _row_spec(tl, LRU_WIDTH)]
    out_shape = [jax.ShapeDtypeStruct((T, 2 * LRU_WIDTH), F32),
                 jax.ShapeDtypeStruct((T, 2 * LRU_WIDTH), F32),
                 jax.ShapeDtypeStruct((T, LRU_WIDTH), BF16)]
    scratch = []
    if pending is not None:
        pos, mod_prev, mf, ys = pending
        last = tl.n - 1
        operands = [pos, pos, mod_prev, mf, ys] + operands
        in_specs = [pl.BlockSpec((2 * tm,), lambda i: (i,), memory_space=pltpu.SMEM),
                    pl.BlockSpec((2 * tm,), lambda i: (jnp.minimum(i + 1, last),), memory_space=pltpu.SMEM),
                    _mod_spec(tl), _row_spec(tl, LANE), pl.BlockSpec(memory_space=pl.ANY)] + in_specs
        out_specs.append(_row_spec(tl, D_MODEL))
        out_shape.append(jax.ShapeDtypeStruct((T, D_MODEL), F32))
        scratch = [pltpu.VMEM((2, 2, tm * SLAB, LANE), F32), pltpu.SemaphoreType.DMA((2,))]
    return pl.pallas_call(
        functools.partial(_lru_in_kernel, pending=pending is not None),
        grid=(tl.n,),
        in_specs=in_specs,
        out_specs=out_specs,
        out_shape=out_shape,
        scratch_shapes=scratch,
        compiler_params=_cparams(),
        name="lru_in",
    )(*operands)


def _lru_scan_kernel(bc_ref, seq_ref, first_ref, last_ref,
                     af_ref, bf_ref, ab_ref, bb_ref, h0_ref, hf_ref, hb_ref, hout_ref, h_s):
    i = pl.program_id(0)
    S = SUBLANE
    W = LRU_WIDTH
    ntile = af_ref.shape[0] // S

    @pl.when(first_ref[i] == 1)
    def _():
        h_s[...] = h0_ref[...]

    ri = lax.broadcasted_iota(I32, (S, W), 0)

    def tile(j, carry):
        hf, hb = carry
        rs = pl.ds(pl.multiple_of(j * S, S), S)
        a = af_ref[rs, 0:W]
        b = bf_ref[rs, 0:W]
        for sh in (1, 2, 4):
            keep = ri >= sh
            a_s = jnp.where(keep, pltpu.roll(a, sh, 0), 1.0)
            b_s = jnp.where(keep, pltpu.roll(b, sh, 0), 0.0)
            b = a * b_s + b
            a = a * a_s
        h8 = a * hf + b
        hf_ref[rs, :] = h8
        hf = h8[S - 1:S]
        rs = pl.ds(pl.multiple_of((ntile - 1 - j) * S, S), S)
        a = ab_ref[rs, W:2 * W]
        b = bb_ref[rs, W:2 * W]
        for sh in (1, 2, 4):
            keep = ri < S - sh
            a_s = jnp.where(keep, pltpu.roll(a, S - sh, 0), 1.0)
            b_s = jnp.where(keep, pltpu.roll(b, S - sh, 0), 0.0)
            b = a * b_s + b
            a = a * a_s
        h8 = a * hb + b
        hb_ref[rs, :] = h8
        hb = h8[0:1]
        return hf, hb

    hf, hb = lax.fori_loop(0, ntile, tile, (h_s[0:1], h_s[1:2]))
    h_s[0:1] = hf
    h_s[1:2] = hb

    @pl.when(last_ref[i] == 1)
    def _():
        hout_ref[...] = h_s[...]


def _lru_scan(lay, a, b, h0):
    B = SCAN_BLK
    T = a.shape[0]
    W = LRU_WIDTH
    tabs = _scan_tables(lay, B)
    nseq = lay.n_ctx + lay.n_lat
    fwd = lambda w: pl.BlockSpec((B, w), lambda i, bc, sq, fi, la_: (i, 0))
    bwd = lambda w: pl.BlockSpec((B, w), lambda i, bc, sq, fi, la_: (bc[i], 0))
    st_spec = pl.BlockSpec((None, 2, W), lambda i, bc, sq, fi, la_: (sq[i], 0, 0))
    grid_spec = pltpu.PrefetchScalarGridSpec(
        num_scalar_prefetch=4,
        grid=(T // B,),
        in_specs=[fwd(2 * W), fwd(2 * W), bwd(2 * W), bwd(2 * W), st_spec],
        out_specs=[fwd(W), bwd(W), st_spec],
        scratch_shapes=[pltpu.VMEM((2, W), F32)],
    )
    return pl.pallas_call(
        _lru_scan_kernel,
        grid_spec=grid_spec,
        out_shape=[
            jax.ShapeDtypeStruct((T, W), F32),
            jax.ShapeDtypeStruct((T, W), F32),
            jax.ShapeDtypeStruct((nseq, 2, W), F32),
        ],
        compiler_params=_cparams(),
        name="lru_scan",
    )(*tabs, a, b, a, b, h0)


def _lru_out_kernel(hf_ref, hb_ref, gg_ref, x_ref, mod_ref, w_ref, *rest):
    *ffn_refs, o_ref = rest
    mod = mod_ref[...]
    y = ((hf_ref[...] + hb_ref[...]) * gg_ref[...].astype(F32)).astype(BF16)
    o_ref[...] = _ffn_tail(x_ref[...] + mod[2:3] * _dot(y, w_ref[...]), mod, ffn_refs)


def _lru_out(tl, hf, hb, gg, x, mod, w_out, ffn=None):
    tl = tl if ffn is not None else tl.wide
    ffn_ops, ffn_specs = _ffn_operands(ffn)
    W = LRU_WIDTH
    return pl.pallas_call(
        _lru_out_kernel,
        grid=(tl.n,),
        in_specs=[_row_spec(tl, W), _row_spec(tl, W), _row_spec(tl, W), _row_spec(tl, D_MODEL),
                  _mod_spec(tl), _full(w_out.shape)] + ffn_specs,
        out_specs=_row_spec(tl, D_MODEL),
        out_shape=jax.ShapeDtypeStruct((tl.T, D_MODEL), F32),
        compiler_params=_cparams(),
        name="lru_out",
    )(hf, hb, gg, x, mod, w_out, *ffn_ops)


def _lru_layer(tl, lay, x, mod, g, h0, w_in, conv_w, conv_b, w_a, b_a, w_x, b_x, lam, w_out,
               ffn=None, pending=None):
    w_gates = jnp.concatenate([w_a[0], w_a[1], w_x[0], w_x[1]], axis=-1).astype(BF16)
    blk = lambda v: v.reshape(2, LRU_BLOCKS, LRU_BLOCK)
    b_gates = jnp.concatenate([blk(b_a)[0], blk(b_a)[1], blk(b_x)[0], blk(b_x)[1]], axis=-1)
    b_gates = b_gates.reshape(LRU_BLOCKS, 1, 4 * LRU_BLOCK)
    outs = _lru_in(tl, lay, x, mod, g.reshape(1, -1), w_in.astype(BF16), conv_w,
                   conv_b.reshape(1, -1), w_gates, b_gates, lam.reshape(1, -1), pending)
    a, b, gg = outs[:3]
    if pending is not None:
        x = outs[3]
    hf, hb, hfin = _lru_scan(lay, a, b, h0)
    x = _lru_out(tl, hf, hb, gg, x, mod, w_out.astype(BF16), ffn)
    return x, hfin


def _swiglu(u, w1_ref, w3_ref, w2_ref):
    a = _dot(u, w1_ref[...])
    b = _dot(u, w3_ref[...])
    return _dot((_silu(a) * b).astype(BF16), w2_ref[...])


def _ffn_kernel(x_ref, mod_ref, g_ref, w1_ref, w3_ref, w2_ref, o_ref):
    x = x_ref[...]
    mod = mod_ref[...]
    u = (_rms(x, g_ref[...]) * (1.0 + mod[4:5]) + mod[3:4]).astype(BF16)
    o_ref[...] = x + mod[5:6] * _swiglu(u, w1_ref, w3_ref, w2_ref)


def _ffn_layer(tl, x, mod, g, w1, w3, w2):
    tl = tl.wide
    w1, w3, w2 = w1.astype(BF16), w3.astype(BF16), w2.astype(BF16)
    return pl.pallas_call(
        _ffn_kernel,
        grid=(tl.n,),
        in_specs=[_row_spec(tl, D_MODEL), _mod_spec(tl), _full((1, D_MODEL)),
                  _full(w1.shape), _full(w3.shape), _full(w2.shape)],
        out_specs=_row_spec(tl, D_MODEL),
        out_shape=jax.ShapeDtypeStruct((tl.T, D_MODEL), F32),
        compiler_params=_cparams(),
        name="ffn",
    )(x, mod, g.reshape(1, -1), w1, w3, w2)


SLAB = D_MODEL // LANE
assert SLAB == SUBLANE


def _to_slab(ref, val):
    tm = val.shape[0]
    for s in range(SLAB):
        ref[pl.ds(s, tm, stride=SLAB), :] = val[:, s * LANE:(s + 1) * LANE]


def _from_slab(ref, tm):
    return jnp.concatenate([ref[pl.ds(s, tm, stride=SLAB), :] for s in range(SLAB)], axis=1)


def _moe_route_kernel(x_ref, mod_ref, g_ref, wr_ref, u_ref, mi_ref, mf_ref, cnt_ref, cnt_s):
    i = pl.program_id(0)
    tm = x_ref.shape[0]

    @pl.when(i == 0)
    def _():
        cnt_s[...] = jnp.zeros_like(cnt_s)

    mod = mod_ref[...]
    u = _rms(x_ref[...], g_ref[...]) * (1.0 + mod[4:5]) + mod[3:4]
    _to_slab(u_ref, u)
    logits = jnp.dot(u, wr_ref[...], preferred_element_type=F32, precision=lax.Precision.HIGHEST)
    lane = lax.broadcasted_iota(I32, (tm, LANE), 1)
    lg = jnp.where(lane < N_EXPERTS, logits, -jnp.inf)
    m1 = jnp.max(lg, axis=1, keepdims=True)
    i1 = jnp.min(jnp.where(lg == m1, lane, LANE), axis=1, keepdims=True)
    lg2 = jnp.where(lane == i1, -jnp.inf, lg)
    m2 = jnp.max(lg2, axis=1, keepdims=True)
    i2 = jnp.min(jnp.where(lg2 == m2, lane, LANE), axis=1, keepdims=True)
    e2 = jnp.exp(m2 - m1)
    g1 = 1.0 / (1.0 + e2)
    g2 = e2 / (1.0 + e2)
    oh1 = lane == i1
    oh2 = lane == i2
    cnt = (oh1 | oh2).astype(BF16)
    rr = lax.broadcasted_iota(I32, (tm, tm), 0)
    cc = lax.broadcasted_iota(I32, (tm, tm), 1)
    before = _dot((rr > cc).astype(BF16), cnt) + cnt_s[...]
    r1 = jnp.sum(jnp.where(oh1, before, 0.0), axis=1, keepdims=True).astype(I32)
    r2 = jnp.sum(jnp.where(oh2, before, 0.0), axis=1, keepdims=True).astype(I32)
    cnt_s[...] = cnt_s[...] + jnp.sum(cnt.astype(F32), axis=0, keepdims=True)
    mi_ref[...] = jnp.where(lane == 0, i1, jnp.where(lane == 1, i2, jnp.where(lane == 2, r1, r2)))
    mf_ref[...] = jnp.where(lane == 0, g1, g2)
    cnt_ref[...] = cnt_s[...]


def _moe_route(tl, x, mod, g, w_router):
    T = tl.T
    return pl.pallas_call(
        _moe_route_kernel,
        grid=(tl.n,),
        in_specs=[_row_spec(tl, D_MODEL), _mod_spec(tl), _full((1, D_MODEL)), _full(w_router.shape)],
        out_specs=[pl.BlockSpec((tl.tm * SLAB, LANE), lambda i: (i, 0)),
                   _row_spec(tl, LANE), _row_spec(tl, LANE),
                   pl.BlockSpec((1, LANE), lambda i: (0, 0))],
        out_shape=[
            jax.ShapeDtypeStruct((T * SLAB, LANE), F32),
            jax.ShapeDtypeStruct((T, LANE), I32),
            jax.ShapeDtypeStruct((T, LANE), F32),
            jax.ShapeDtypeStruct((1, LANE), F32),
        ],
        scratch_shapes=[pltpu.VMEM((1, LANE), F32)],
        compiler_params=_cparams(),
        name="moe_route",
    )(x, mod, g, w_router)


DISPATCH_BLK = 512


def _moe_dispatch_kernel(pos_ref, tail_ref, u_ref, z_hbm, xs_hbm, sem, *, nblk, tm, nt):
    i = pl.program_id(0)

    def rows(t, n=1):
        return pl.ds(pl.multiple_of(t * SLAB, SLAB), n * SLAB)

    def issue(r, carry):
        for k in range(2):
            pltpu.make_async_copy(u_ref.at[rows(r)], xs_hbm.at[rows(pos_ref[2 * r + k])],
                                  sem).start(priority=k)
        return carry

    lax.fori_loop(0, nblk, issue, 0, unroll=8)
    for k in range(2):
        pltpu.make_async_copy(u_ref, xs_hbm.at[rows(0, nblk)], sem).wait()

    @pl.when(i == pl.num_programs(0) - 1)
    def _():
        for e in range(N_EXPERTS):
            start = tail_ref[e]
            n = tail_ref[N_EXPERTS + e]

            def fill(r, carry):
                pltpu.make_async_copy(z_hbm.at[rows(0)], xs_hbm.at[rows(start + r)], sem).start()
                return carry

            def drain(r, carry):
                pltpu.make_async_copy(z_hbm.at[rows(0)], xs_hbm.at[rows(start + r)], sem).wait()
                return carry

            lax.fori_loop(0, n, fill, 0)
            lax.fori_loop(0, n, drain, 0)

        def fill_tile(j, carry):
            dst = xs_hbm.at[rows(j * tm, tm)]
            pltpu.make_async_copy(z_hbm, dst, sem).start()
            pltpu.make_async_copy(z_hbm, dst, sem).wait()
            return carry

        lax.fori_loop(tail_ref[2 * N_EXPERTS], nt, fill_tile, 0)


def _moe_dispatch(T, P, tm, pos, tail, u_slab):
    nblk = DISPATCH_BLK
    assert T % nblk == 0 and P % tm == 0
    kern = functools.partial(_moe_dispatch_kernel, nblk=nblk, tm=tm, nt=P // tm)
    zeros = jnp.zeros((tm * SLAB, LANE), F32)
    return pl.pallas_call(
        kern,
        grid=(T // nblk,),
        in_specs=[
            pl.BlockSpec((2 * nblk,), lambda i: (i,), memory_space=pltpu.SMEM),
            pl.BlockSpec(memory_space=pltpu.SMEM),
            pl.BlockSpec((nblk * SLAB, LANE), lambda i: (i, 0)),
            pl.BlockSpec(memory_space=pl.ANY),
        ],
        out_specs=pl.BlockSpec(memory_space=pl.ANY),
        out_shape=jax.ShapeDtypeStruct((P * SLAB, LANE), F32),
        scratch_shapes=[pltpu.SemaphoreType.DMA(())],
        compiler_params=_cparams(),
        name="moe_dispatch",
    )(pos, tail, u_slab, zeros)


def _moe_ffn_kernel(texp_ref, nv_ref, xs_ref, w1_ref, w3_ref, w2_ref, ys_ref):
    j = pl.program_id(0)
    tm = xs_ref.shape[0] // SLAB
    valid = j < nv_ref[0]

    @pl.when(valid)
    def _():
        u = _from_slab(xs_ref, tm).astype(BF16)
        _to_slab(ys_ref, _swiglu(u, w1_ref, w3_ref, w2_ref))

    @pl.when(jnp.logical_not(valid))
    def _():
        ys_ref[...] = jnp.zeros_like(ys_ref)


def _moe_ffn(tm, nt, texp, nv, xs2d, w1, w3, w2):
    wspec = lambda shp: pl.BlockSpec((None,) + shp, lambda j, te, nv_: (te[j], 0, 0))
    grid_spec = pltpu.PrefetchScalarGridSpec(
        num_scalar_prefetch=2,
        grid=(nt,),
        in_specs=[
            pl.BlockSpec((tm * SLAB, LANE), lambda j, te, nv_: (jnp.minimum(j, nv_[0] - 1), 0)),
            wspec(w1.shape[1:]), wspec(w3.shape[1:]), wspec(w2.shape[1:]),
        ],
        out_specs=pl.BlockSpec((tm * SLAB, LANE), lambda j, te, nv_: (j, 0)),
    )
    return pl.pallas_call(
        _moe_ffn_kernel,
        grid_spec=grid_spec,
        out_shape=jax.ShapeDtypeStruct(xs2d.shape, F32),
        compiler_params=_cparams(),
        name="moe_ffn",
    )(texp, nv, xs2d, w1, w3, w2)


def _moe_combine_kernel(pos_ref, x_ref, mod_ref, mf_ref, ys_hbm, o_ref, buf0, buf1, sem):
    tm = x_ref.shape[0]
    bufs = (buf0, buf1)

    def issue(r, carry):
        for k in range(2):
            dst = bufs[k].at[pl.ds(pl.multiple_of(r * SLAB, SLAB), SLAB)]
            pltpu.make_async_copy(ys_hbm.at[pos_ref[2 * r + k]], dst, sem).start(priority=k)
        return carry

    lax.fori_loop(0, tm, issue, 0, unroll=8)
    for k in range(2):
        pltpu.make_async_copy(bufs[1 - k], bufs[k], sem).wait()
    mf = mf_ref[...]
    g1 = mf[:, 0:1]
    g2 = mf[:, 1:2]
    for s in range(SLAB):
        cs = slice(s * LANE, (s + 1) * LANE)
        y = g1 * buf0[pl.ds(s, tm, stride=SLAB), :] + g2 * buf1[pl.ds(s, tm, stride=SLAB), :]
        o_ref[:, cs] = x_ref[:, cs] + mod_ref[5:6, cs] * y


def _moe_combine(tl, pos, x, mod, mf, ys):
    tm = tl.tm
    return pl.pallas_call(
        _moe_combine_kernel,
        grid=(tl.n,),
        in_specs=[
            pl.BlockSpec((2 * tm,), lambda i: (i,), memory_space=pltpu.SMEM),
            _row_spec(tl, D_MODEL), _mod_spec(tl), _row_spec(tl, LANE),
            pl.BlockSpec(memory_space=pl.ANY),
        ],
        out_specs=_row_spec(tl, D_MODEL),
        out_shape=jax.ShapeDtypeStruct((tl.T, D_MODEL), F32),
        scratch_shapes=[pltpu.VMEM((tm * SLAB, LANE), F32), pltpu.VMEM((tm * SLAB, LANE), F32),
                        pltpu.SemaphoreType.DMA(())],
        compiler_params=_cparams(),
        name="moe_combine",
    )(pos, x, mod, mf, ys)


def _moe_layer(tl, x, mod, g, w_router, w1, w3, w2, defer=False):
    T, tm = tl.T, tl.tm
    nt = 2 * tl.n + N_EXPERTS
    P = nt * tm
    wr = jnp.pad(w_router, ((0, 0), (0, LANE - N_EXPERTS)))
    u_slab, mi, mf, cnt = _moe_route(tl, x, mod, g.reshape(1, -1), wr)
    counts = cnt[0, :N_EXPERTS].astype(I32)
    tiles_e = (counts + tm - 1) // tm
    cum_tiles = jnp.cumsum(tiles_e)
    nv = cum_tiles[-1:]
    off = (cum_tiles - tiles_e) * tm
    sel = mi[:, 0:2, None] == jnp.arange(N_EXPERTS, dtype=I32)
    pos = (jnp.sum(jnp.where(sel, off, 0), axis=-1) + mi[:, 2:4]).reshape(2 * T)
    jj = jnp.minimum(jnp.arange(nt, dtype=I32), nv[0] - 1)
    texp = jnp.sum((jj[:, None] >= cum_tiles[None, :]).astype(I32), axis=1)
    tail = jnp.concatenate([off + counts, tiles_e * tm - counts, nv])
    xs = _moe_dispatch(T, P, tm, pos, tail, u_slab)
    ys = _moe_ffn(tm, nt, texp, nv, xs,
                  w1.astype(BF16), w3.astype(BF16), w2.astype(BF16))
    ys = ys.reshape(P, SLAB, LANE)
    if defer:
        return x, (pos, mod, mf, ys)
    return _moe_combine(tl, pos, x, mod, mf, ys), None


def _final_kernel(x_ref, g_ref, o_ref):
    o_ref[...] = _rms(x_ref[...], g_ref[...])


def _final_norm(tl, x, g, first_tile, n_tiles):
    return pl.pallas_call(
        _final_kernel,
        grid=(n_tiles,),
        in_specs=[pl.BlockSpec((tl.tm, D_MODEL), lambda i: (i + first_tile, 0)), _full((1, D_MODEL))],
        out_specs=_row_spec(tl, D_MODEL),
        out_shape=jax.ShapeDtypeStruct((n_tiles * tl.tm, D_MODEL), F32),
        compiler_params=_cparams(),
        name="final_norm",
    )(x, g.reshape(1, -1))


def kernel(x_prompt, x_sample, state_ssd, state_gla, state_lru, c, c_ctx, ada_w, ada_b, norm_g, final_g, ssd_w_in, ssd_conv_w, ssd_conv_b, ssd_a_log, ssd_dt_bias, ssd_d, ssd_norm_g, ssd_w_out, gla_w_in, gla_w_gate_up, gla_b_gate, gla_norm_g, gla_w_out, lru_w_in, lru_conv_w, lru_conv_b, lru_w_a, lru_b_a, lru_w_x, lru_b_x, lru_lambda, lru_w_out, ffn_w1, ffn_w3, ffn_w2, moe_router, moe_w1, moe_w3, moe_w2):
    n_ctx, len_ctx, _ = x_prompt.shape
    n_lat, len_lat, _ = x_sample.shape
    lay = Layout(n_ctx, len_ctx, n_lat, len_lat)
    tl = _Tiles(lay, TM)
    tl2 = tl.wide
    assert len_ctx & (len_ctx - 1) == 0 and len_ctx <= TM and TM % GRID_W == 0

    x = jnp.concatenate([x_prompt.reshape(-1, D_MODEL), x_sample.reshape(-1, D_MODEL)], axis=0)
    c_all = jnp.concatenate([c_ctx[None], c], axis=0)
    c_all = jnp.pad(c_all, ((0, -c_all.shape[0] % SUBLANE), (0, 0)))
    mods = _mods(c_all, ada_w, ada_b)

    caches = (state_ssd, state_gla, state_lru)
    ctx_states = ([], [], [])
    pending = None
    for i in range(DEPTH):
        kind, j = i % N_MIXERS, i // N_MIXERS
        cache = caches[kind][:, j]
        h0 = jnp.concatenate([jnp.zeros((n_ctx,) + cache.shape[1:], F32), cache], axis=0)
        dense = (norm_g[i, 1], ffn_w1[i // 2], ffn_w3[i // 2], ffn_w2[i // 2]) if i % 2 == 0 else None
        if kind == 0:
            x, hfin = _ssd_layer(tl, lay, x, mods[i], norm_g[i, 0], h0, ssd_w_in[j], ssd_conv_w[j],
                                 ssd_conv_b[j], ssd_a_log[j], ssd_dt_bias[j], ssd_d[j],
                                 ssd_norm_g[j], ssd_w_out[j], ffn=dense)
        elif kind == 1:
            x, hfin = _gla_layer(tl, lay, x, mods[i], norm_g[i, 0], h0, gla_w_in[j], gla_w_gate_up[j],
                                 gla_b_gate[j], gla_norm_g[j], gla_w_out[j])
        else:
            x, hfin = _lru_layer(tl, lay, x, mods[i], norm_g[i, 0], h0, lru_w_in[j], lru_conv_w[j],
                                 lru_conv_b[j], lru_w_a[j], lru_b_a[j], lru_w_x[j], lru_b_x[j],
                                 lru_lambda[j], lru_w_out[j], ffn=dense, pending=pending)
            pending = None
        assert pending is None
        ctx_states[kind].append(hfin[:n_ctx])
        if i % 2 == 0:
            if kind == 1:
                x = _ffn_layer(tl, x, mods[i], *dense)
        else:
            next_is_lru = i + 1 < DEPTH and (i + 1) % N_MIXERS == 2
            x, pending = _moe_layer(tl, x, mods[i], norm_g[i, 1], moe_router[i // 2], moe_w1[i // 2],
                                    moe_w3[i // 2], moe_w2[i // 2], defer=next_is_lru)

    y_ctx = _final_norm(tl2, x, final_g, 0, tl2.nct)
    y_lat = _final_norm(tl2, x, final_g, tl2.nct, tl2.n - tl2.nct)
    return (y_ctx.reshape(n_ctx, len_ctx, D_MODEL),
            y_lat.reshape(n_lat, len_lat, D_MODEL),
            jnp.stack(ctx_states[0], axis=1),
            jnp.stack(ctx_states[1], axis=1),
            jnp.stack(ctx_states[2], axis=1))
```
